```python
import jax, jax.numpy as jnp
from jax import lax
import numpy as np

D_MODEL = 1024
BATCH = 4
SEQ = 8192
DEPTH = 2

PLE_DIM = 256
D_MIX = D_MODEL
HEAD_DIM = 64
RMS_EPS = 1e-6

CONV_W = D_MIX // 4
CONV_K = 3

NSA_HEADS = 6
NSA_KV_HEADS = 2
NSA_GROUP = NSA_HEADS // NSA_KV_HEADS
NSA_W = NSA_HEADS * HEAD_DIM
NSA_KV_W = NSA_KV_HEADS * HEAD_DIM
CMP_BLOCK = 32
CMP_STRIDE = 16
SLC_BLOCK = 64
SLC_TOPK = 16
WIN = 512
Q_BLOCK = 128
FORCE_SCORE = 1e4

DIL_PAIRS = ((128, 1), (512, 4), (2048, 16))
DIL_HEADS_PER_PAIR = 2
DIL_HEADS = DIL_HEADS_PER_PAIR * len(DIL_PAIRS)
DIL_W = DIL_HEADS * HEAD_DIM
DIL_BLOCK = 128

IN_SIZES = ((CONV_W,) * 4
            + (NSA_W,) + (NSA_KV_W,) * 6
            + (NSA_HEADS * 3, NSA_W)
            + (DIL_W,) * 4)
D_IN = sum(IN_SIZES)

kernel_name = "hymba_conv_nsa_dilated_hybrid"


def _rmsnorm(x, g):
    x32 = x.astype(jnp.float32)
    y = x32 * lax.rsqrt(jnp.mean(x32 * x32, axis=-1, keepdims=True) + RMS_EPS)
    return (y * g.astype(jnp.float32)).astype(x.dtype)


def _masked_softmax(s, mask):
    s = jnp.where(mask, s.astype(jnp.float32), -jnp.inf)
    m = jnp.max(s, axis=-1, keepdims=True)
    m = jnp.where(jnp.isfinite(m), m, 0.0)
    e = jnp.where(mask, jnp.exp(s - m), 0.0)
    den = jnp.sum(e, axis=-1, keepdims=True)
    return e / jnp.maximum(den, 1e-30), m, den


def _short_conv(u, w, b):
    y = lax.conv_general_dilated(
        u, w[:, None, :].astype(u.dtype), window_strides=(1,), padding=((CONV_K - 1, 0),),
        dimension_numbers=("NWC", "WIO", "NWC"), feature_group_count=u.shape[-1])
    return y + b.astype(u.dtype)


def _nsa(q, kc_raw, vc_raw, ks_raw, vs_raw, kw_raw, vw_raw, gates, pe, w1, w2):
    B, T, G, R, E = q.shape
    scale = E ** -0.5
    n_cmp = (T - CMP_BLOCK) // CMP_STRIDE + 1
    cmp_start = jnp.arange(n_cmp) * CMP_STRIDE
    cmp_idx = cmp_start[:, None] + jnp.arange(CMP_BLOCK)[None, :]

    def compress(a, pe_, w1_, w2_):
        blk = a[:, cmp_idx] + pe_[None, None, :, None, :].astype(a.dtype)
        flat = jnp.moveaxis(blk, 3, 2).reshape(B, n_cmp, G, CMP_BLOCK * E)
        return jax.nn.silu(flat @ w1_) @ w2_

    k_cmp = compress(kc_raw, pe[0], w1[0], w2[0])
    v_cmp = compress(vc_raw, pe[1], w1[1], w2[1])
    cmp_last = cmp_start + CMP_BLOCK - 1
    n_slc = T // SLC_BLOCK
    slc_start = jnp.arange(n_slc) * SLC_BLOCK
    overlap = ((cmp_start[:, None] < slc_start[None, :] + SLC_BLOCK)
               & (cmp_start[:, None] + CMP_BLOCK > slc_start[None, :])).astype(jnp.float32)
    k_slc = ks_raw.reshape(B, n_slc, SLC_BLOCK, G, E).transpose(0, 3, 1, 2, 4)
    v_slc = vs_raw.reshape(B, n_slc, SLC_BLOCK, G, E).transpose(0, 3, 1, 2, 4)
    top = min(SLC_TOPK, n_slc)
    b_ix = jnp.arange(B)[:, None, None, None]
    g_ix = jnp.arange(G)[None, None, :, None]
    k_win = jnp.pad(kw_raw, ((0, 0), (WIN, 0), (0, 0), (0, 0)))
    v_win = jnp.pad(vw_raw, ((0, 0), (WIN, 0), (0, 0), (0, 0)))

    def block(qi):
        q0 = qi * Q_BLOCK
        t = q0 + jnp.arange(Q_BLOCK)
        qb = lax.dynamic_slice_in_dim(q, q0, Q_BLOCK, axis=1)
        gb = lax.dynamic_slice_in_dim(gates, q0, Q_BLOCK, axis=1)
        s_c = jnp.einsum("bqgrd,bngd->bqgrn", qb, k_cmp) * scale
        m_c = (cmp_last[None, :] <= t[:, None])[None, :, None, None, :]
        p_c, _, _ = _masked_softmax(s_c, m_c)
        o_c = jnp.einsum("bqgrn,bngd->bqgrd", p_c.astype(qb.dtype), v_cmp)
        imp = jnp.einsum("bqgrn,ns->bqgs", p_c, overlap)
        blk = jnp.arange(n_slc)[None, :]
        cur = (t // SLC_BLOCK)[:, None]
        forced = (blk == 0) | (blk == cur) | (blk == cur - 1)
        causal = blk * SLC_BLOCK <= t[:, None]
        imp = jnp.where(forced[None, :, None, :], FORCE_SCORE, imp)
        imp = jnp.where(causal[None, :, None, :], imp, -1.0)
        _, sel = lax.top_k(imp, top)
        k_sel = k_slc[b_ix, g_ix, sel]
        v_sel = v_slc[b_ix, g_ix, sel].reshape(B, Q_BLOCK, G, top * SLC_BLOCK, E)
        pos = sel[..., None] * SLC_BLOCK + jnp.arange(SLC_BLOCK)
        m_s = (pos <= t[None, :, None, None, None]).reshape(B, Q_BLOCK, G, 1, top * SLC_BLOCK)
        s_s = jnp.einsum("bqgrd,bqgkld->bqgrkl", qb, k_sel).reshape(
            B, Q_BLOCK, G, R, top * SLC_BLOCK) * scale
        p_s, _, _ = _masked_softmax(s_s, m_s)
        o_s = jnp.einsum("bqgrk,bqgkd->bqgrd", p_s.astype(qb.dtype), v_sel)
        kw = lax.dynamic_slice_in_dim(k_win, q0, WIN + Q_BLOCK, axis=1)
        vw = lax.dynamic_slice_in_dim(v_win, q0, WIN + Q_BLOCK, axis=1)
        kpos = q0 - WIN + jnp.arange(WIN + Q_BLOCK)
        dist = t[:, None] - kpos[None, :]
        m_w = ((dist >= 0) & (dist < WIN) & (kpos[None, :] >= 0))[None, :, None, None, :]
        s_w = jnp.einsum("bqgrd,bkgd->bqgrk", qb, kw) * scale
        p_w, _, _ = _masked_softmax(s_w, m_w)
        o_w = jnp.einsum("bqgrk,bkgd->bqgrd", p_w.astype(qb.dtype), vw)
        return gb[..., 0:1] * o_c + gb[..., 1:2] * o_s + gb[..., 2:3] * o_w

    out = lax.map(block, jnp.arange(T // Q_BLOCK))
    return jnp.moveaxis(out, 0, 1).reshape(B, T, G * R * E)


def _dilated_group(q, k, v, window, dil):
    B, T, H, E = q.shape
    L = T // dil
    Lp = -(-L // DIL_BLOCK) * DIL_BLOCK
    nb = Lp // DIL_BLOCK
    steps = window // dil

    def sub(a):
        a = a.reshape(B, L, dil, H, E).transpose(0, 2, 3, 1, 4)
        a = jnp.pad(a, ((0, 0), (0, 0), (0, 0), (0, Lp - L), (0, 0)))
        return a.reshape(B, dil, H, nb, DIL_BLOCK, E)

    def with_prev(a):
        prev = jnp.concatenate([jnp.zeros_like(a[:, :, :, :1]), a[:, :, :, :-1]], axis=3)
        return jnp.concatenate([prev, a], axis=4)

    qs = sub(q)
    kc = with_prev(sub(k))
    vc = with_prev(sub(v))
    s = jnp.einsum("bdhnqe,bdhnke->bdhnqk", qs, kc) * (E ** -0.5)
    a_i = jnp.arange(DIL_BLOCK)
    c_i = jnp.arange(2 * DIL_BLOCK)
    dist = a_i[:, None] + DIL_BLOCK - c_i[None, :]
    j = jnp.arange(nb)[:, None, None] * DIL_BLOCK - DIL_BLOCK + c_i[None, None, :]
    mask = (dist[None] >= 0) & (dist[None] <= steps) & (j >= 0)
    p, m, den = _masked_softmax(s, mask)
    o = jnp.einsum("bdhnqk,bdhnke->bdhnqe", p.astype(v.dtype), vc)
    lse = (m + jnp.log(den))[..., 0]
    o = o.reshape(B, dil, H, Lp, E)[:, :, :, :L].transpose(0, 3, 1, 2, 4).reshape(B, T, H, E)
    lse = lse.reshape(B, dil, H, Lp)[:, :, :, :L].transpose(0, 3, 1, 2).reshape(B, T, H)
    return o, lse


def _dilated_mixer(q, k, v):
    B, T, _, E = q.shape
    outs, lses = [], []
    for gi, (window, dil) in enumerate(DIL_PAIRS):
        sl = slice(gi * DIL_HEADS_PER_PAIR, (gi + 1) * DIL_HEADS_PER_PAIR)
        o, lse = _dilated_group(q[:, :, sl], k[:, :, sl], v[:, :, sl], window, dil)
        outs.append(o)
        lses.append(lse)
    alpha = jax.nn.softmax(jnp.stack(lses, axis=0), axis=0)
    y = jnp.concatenate([alpha[gi][..., None].astype(outs[gi].dtype) * outs[gi]
                         for gi in range(len(DIL_PAIRS))], axis=2)
    return y.reshape(B, T, DIL_W)


def setup_inputs(seed: int = 0) -> dict:
    key = jax.random.key(seed)
    ks = jax.random.split(key, 13)
    f32 = jnp.float32
    x = jax.random.normal(ks[0], (BATCH, SEQ, D_MODEL), f32)
    p = jax.random.normal(ks[1], (DEPTH, BATCH, SEQ, PLE_DIM), f32)
    norm_mix = 1.0 + 0.05 * jax.random.normal(ks[2], (DEPTH, D_MODEL), f32)
    w_in = jax.random.normal(ks[3], (DEPTH, D_MODEL, D_IN), f32) * D_MODEL ** -0.5
    conv_w = jax.random.normal(ks[4], (DEPTH, CONV_K, CONV_W), f32) * CONV_K ** -0.5
    conv_b = 0.02 * jax.random.normal(ks[5], (DEPTH, CONV_W), f32)
    cmp_pe = 0.5 * jax.random.normal(ks[6], (DEPTH, 2, CMP_BLOCK, HEAD_DIM), f32)
    cmp_w1 = jax.random.normal(ks[7], (DEPTH, 2, CMP_BLOCK * HEAD_DIM, HEAD_DIM), f32) * (CMP_BLOCK * HEAD_DIM) ** -0.5
    cmp_w2 = jax.random.normal(ks[8], (DEPTH, 2, HEAD_DIM, HEAD_DIM), f32) * HEAD_DIM ** -0.5
    w_out = jax.random.normal(ks[9], (DEPTH, D_MIX, D_MODEL), f32) * D_MIX ** -0.5
    w_ple_gate = jax.random.normal(ks[10], (DEPTH, D_MODEL, D_MODEL), f32) * D_MODEL ** -0.5
    w_ple_proj = jax.random.normal(ks[11], (DEPTH, PLE_DIM, D_MODEL), f32) * PLE_DIM ** -0.5
    norm_final = 1.0 + 0.05 * jax.random.normal(ks[12], (D_MODEL,), f32)
    return {"x": x, "p": p, "norm_mix": norm_mix, "w_in": w_in, "conv_w": conv_w,
            "conv_b": conv_b, "cmp_pe": cmp_pe, "cmp_w1": cmp_w1, "cmp_w2": cmp_w2,
            "w_out": w_out, "w_ple_gate": w_ple_gate, "w_ple_proj": w_ple_proj,
            "norm_final": norm_final}


def reference(x, p, norm_mix, w_in, conv_w, conv_b, cmp_pe, cmp_w1, cmp_w2,
              w_out, w_ple_gate, w_ple_proj, norm_final):
    B, T, _ = x.shape
    E = HEAD_DIM
    split_points = np.cumsum(IN_SIZES)[:-1].tolist()
    h = x
    for i in range(DEPTH):
        xn = _rmsnorm(h, norm_mix[i])
        u = xn @ w_in[i]
        (a_b, a_c, a_h, a_z,
         n_q, n_kc, n_vc, n_ks, n_vs, n_kw, n_vw, n_g, n_z,
         d_q, d_k, d_v, d_z) = jnp.split(u, split_points, axis=-1)
        y_a = a_b * _short_conv(a_c * a_h, conv_w[i], conv_b[i]) * jax.nn.silu(a_z)
        kv = lambda a: a.reshape(B, T, NSA_KV_HEADS, E)
        gates = jax.nn.sigmoid(n_g.astype(jnp.float32)).astype(u.dtype).reshape(
            B, T, NSA_KV_HEADS, NSA_GROUP, 3)
        y_b = _nsa(n_q.reshape(B, T, NSA_KV_HEADS, NSA_GROUP, E), kv(n_kc), kv(n_vc),
                   kv(n_ks), kv(n_vs), kv(n_kw), kv(n_vw), gates,
                   cmp_pe[i], cmp_w1[i], cmp_w2[i]) * jax.nn.silu(n_z)
        hd = lambda a: a.reshape(B, T, DIL_HEADS, E)
        y_c = _dilated_mixer(hd(d_q), hd(d_k), hd(d_v)) * jax.nn.silu(d_z)
        h = h + jnp.concatenate([y_a, y_b, y_c], axis=-1) @ w_out[i]
        gate = jax.nn.sigmoid((h @ w_ple_gate[i]).astype(jnp.float32)).astype(h.dtype)
        h = h + gate * (p[i] @ w_ple_proj[i])
    return _rmsnorm(h, norm_final)
```

```python
import functools

import numpy as np
import jax
import jax.numpy as jnp
from jax import lax
from jax.experimental import pallas as pl
from jax.experimental.pallas import tpu as pltpu

F32 = jnp.float32
BF16 = jnp.bfloat16

D_MODEL = 1024
PLE_DIM = 256
HEAD_DIM = 64
RMS_EPS = 1e-6
CONV_W = 256
CONV_K = 3
NSA_HEADS = 6
NSA_KV_HEADS = 2
NSA_GROUP = 3
NSA_W = 384
CMP_BLOCK = 32
CMP_STRIDE = 16
SLC_BLOCK = 64
SLC_TOPK = 16
WIN = 512
FORCE_SCORE = 1e4
DIL_PAIRS = ((128, 1), (512, 4), (2048, 16))
DIL_W = 384
DIL_BLOCK = 128
IN_SIZES = (256,) * 4 + (384,) + (128,) * 6 + (18, 384) + (384,) * 4

LANES = 128
N_SLC_PAD = 128
Q_TILE = 128
KV_TILE = 256
ROW_TILE = 512
NEG = -1e30
SCALE = HEAD_DIM ** -0.5
VMEM_LIMIT = 56 * 1024 * 1024

_NT = (((1,), (1,)), ((), ()))


def _cparams(sem):
    return pltpu.CompilerParams(dimension_semantics=sem, vmem_limit_bytes=VMEM_LIMIT)


_C_CONV = 0
_C_QA = 1024
_C_KS = _C_QA + 768
_C_VS = _C_KS + 256
_C_KW = _C_VS + 256
_C_VW = _C_KW + 256
_C_KC = _C_VW + 256
_C_VC = _C_KC + 128
_C_NG = _C_VC + 128
_C_NZ = _C_NG + 128
_C_DQ = _C_NZ + 384
_C_DK = _C_DQ + 384
_C_DV = _C_DK + 384
_C_DZ = _C_DV + 384
_C_END = _C_DZ + 384


def _permute_w_in(w):
    offs = np.concatenate([[0], np.cumsum(IN_SIZES)])
    seg = lambda k: w[:, int(offs[k]):int(offs[k + 1])]
    z64 = jnp.zeros((w.shape[0], HEAD_DIM), w.dtype)

    def pad_heads(a, n, scale=None):
        cols = []
        for h in range(n):
            c = a[:, h * HEAD_DIM:(h + 1) * HEAD_DIM]
            cols += [c * scale if scale is not None else c, z64]
        return cols

    cols = [seg(0), seg(1), seg(2), seg(3)]
    cols += pad_heads(seg(4), NSA_HEADS, SCALE)
    cols += pad_heads(seg(7), 2) + pad_heads(seg(8), 2)
    cols += pad_heads(seg(9), 2) + pad_heads(seg(10), 2)
    cols += [seg(5), seg(6)]
    cols += [seg(11), jnp.zeros((w.shape[0], LANES - 18), w.dtype)]
    cols += [seg(12), seg(13) * SCALE, seg(14), seg(15), seg(16)]
    out = jnp.concatenate(cols, axis=1)
    assert out.shape[1] == _C_END
    return out.astype(BF16)


def _inproj_body(x_ref, g_ref, w_ref, conv_ref, qa_ref, kaug_ref, vsp_ref, kwp_ref, vwp_ref,
                 kc_ref, vc_ref, ng_ref, nz_ref, dq_ref, dk_ref, dv_ref, dz_ref, *, tm):
    x = x_ref[0]
    ms = jnp.mean(x * x, axis=-1, keepdims=True)
    xn = (x * lax.rsqrt(ms + RMS_EPS) * g_ref[...]).astype(BF16)

    def seg(a, b):
        return jnp.dot(xn, w_ref[:, a:b], preferred_element_type=F32)

    conv_ref[0] = seg(_C_CONV, _C_QA)
    q = seg(_C_QA, _C_KS)
    for h in range(NSA_HEADS):
        qa_ref[0, h] = q[:, h * LANES:(h + 1) * LANES].astype(BF16)

    t0 = pl.program_id(1) * tm
    row = lax.broadcasted_iota(jnp.int32, (tm, LANES), 0) + t0
    lane = lax.broadcasted_iota(jnp.int32, (tm, LANES), 1)
    onehot = jnp.where((row >> 6) == lane, 1.0, 0.0).astype(BF16)
    ones_hi = lane >= HEAD_DIM

    kv = seg(_C_KS, _C_KC)
    for g in range(NSA_KV_HEADS):
        ks = kv[:, g * LANES:(g + 1) * LANES].astype(BF16)
        kaug_ref[0, g] = jnp.concatenate([onehot, ks], axis=1)
        vs = kv[:, 256 + g * LANES:256 + (g + 1) * LANES]
        vsp_ref[0, g] = jnp.where(ones_hi, 1.0, vs).astype(BF16)
        kwp_ref[0, g] = kv[:, 512 + g * LANES:512 + (g + 1) * LANES].astype(BF16)
        vw = kv[:, 768 + g * LANES:768 + (g + 1) * LANES]
        vwp_ref[0, g] = jnp.where(ones_hi, 1.0, vw).astype(BF16)

    misc = seg(_C_KC, _C_DQ)
    kc_ref[0] = misc[:, 0:128]
    vc_ref[0] = misc[:, 128:256]
    ng_ref[0] = misc[:, 256:384]
    nz_ref[0] = misc[:, 384:768]
    dil = seg(_C_DQ, _C_END)
    dq_ref[0] = dil[:, 0:384].astype(BF16)
    dk_ref[0] = dil[:, 384:768].astype(BF16)
    dv_ref[0] = dil[:, 768:1152].astype(BF16)
    dz_ref[0] = dil[:, 1152:1536]


def _inproj(h, g, w):
    B, T, D = h.shape
    tm = min(ROW_TILE, T)
    assert T % tm == 0
    row = lambda n: pl.BlockSpec((1, tm, n), lambda b, i: (b, i, 0))
    head = lambda nh, n: pl.BlockSpec((1, nh, tm, n), lambda b, i: (b, 0, i, 0))
    sd = jax.ShapeDtypeStruct
    out_shape = (
        sd((B, T, 1024), F32), sd((B, 6, T, LANES), BF16), sd((B, 2, T, 2 * LANES), BF16),
        sd((B, 2, T, LANES), BF16), sd((B, 2, T, LANES), BF16), sd((B, 2, T, LANES), BF16),
        sd((B, T, 128), F32), sd((B, T, 128), F32), sd((B, T, 128), F32), sd((B, T, 384), F32),
        sd((B, T, 384), BF16), sd((B, T, 384), BF16), sd((B, T, 384), BF16), sd((B, T, 384), F32))
    out_specs = (row(1024), head(6, LANES), head(2, 2 * LANES), head(2, LANES), head(2, LANES),
                 head(2, LANES), row(128), row(128), row(128), row(384),
                 row(384), row(384), row(384), row(384))
    return pl.pallas_call(
        functools.partial(_inproj_body, tm=tm),
        grid=(B, T // tm),
        in_specs=[row(D), pl.BlockSpec((1, D), lambda b, i: (0, 0)),
                  pl.BlockSpec((D, _C_END), lambda b, i: (0, 0))],
        out_specs=out_specs, out_shape=out_shape,
        compiler_params=_cparams(("arbitrary", "arbitrary")),
        name="inproj",
    )(h, g, w)


def _compress_weights(pe, w1, w2):
    eye = jnp.eye(NSA_KV_HEADS, dtype=F32)
    pe4 = pe.reshape(2, 2, 16, 1, HEAD_DIM)
    pe_t = jnp.broadcast_to(pe4, (2, 2, 16, NSA_KV_HEADS, HEAD_DIM)).reshape(2, 2, 1, 2048)
    w1r = w1.reshape(2, 2, 16, HEAD_DIM, HEAD_DIM)
    w1b = jnp.einsum("khmeo,gf->khmgefo", w1r, eye).reshape(2, 2, 2048, 128)
    w2k = jnp.zeros((NSA_KV_HEADS, 128, 128), F32)
    for g in range(NSA_KV_HEADS):
        w2k = w2k.at[g, g * 64:(g + 1) * 64, 0:64].set(w2[0])
    w2v = jnp.einsum("eo,gf->gefo", w2[1], eye).reshape(128, 128)
    return pe_t.astype(F32), w1b.astype(BF16), w2k.astype(BF16), w2v.astype(BF16)


def _compress_body(kc_ref, vc_ref, pe_ref, w1_ref, w2k_ref, w2v_ref, kcmp_ref, vcmpT_ref, *, nc):
    def hidden(z, kv):
        a0 = jnp.dot((z + pe_ref[kv, 0]).astype(BF16), w1_ref[kv, 0], preferred_element_type=F32)
        a1 = jnp.dot((z + pe_ref[kv, 1]).astype(BF16), w1_ref[kv, 1], preferred_element_type=F32)
        hid = a0 + pltpu.roll(a1, nc - 1, 0)
        return jax.nn.silu(hid).astype(BF16)

    hk = hidden(kc_ref[0], 0)
    for g in range(NSA_KV_HEADS):
        kcmp_ref[0, g] = jnp.dot(hk, w2k_ref[g], preferred_element_type=F32).astype(BF16)
    hv = hidden(vc_ref[0], 1)
    vout = jnp.dot(hv, w2v_ref[...], preferred_element_type=F32)
    vcmpT_ref[0] = vout.T.astype(BF16)


def _compress(kc, vc, pe_t, w1b, w2k, w2v):
    B, T, _ = kc.shape
    nc = T // CMP_STRIDE
    z = lambda a: a.reshape(B, nc, 2048)
    full = lambda shape: pl.BlockSpec(shape, lambda b: (0,) * len(shape))
    return pl.pallas_call(
        functools.partial(_compress_body, nc=nc),
        grid=(B,),
        in_specs=[pl.BlockSpec((1, nc, 2048), lambda b: (b, 0, 0)),
                  pl.BlockSpec((1, nc, 2048), lambda b: (b, 0, 0)),
                  full((2, 2, 1, 2048)), full((2, 2, 2048, 128)), full((2, 128, 128)), full((128, 128))],
        out_specs=(pl.BlockSpec((1, 2, nc, 128), lambda b: (b, 0, 0, 0)),
                   pl.BlockSpec((1, 128, nc), lambda b: (b, 0, 0))),
        out_shape=(jax.ShapeDtypeStruct((B, 2, nc, 128), BF16),
                   jax.ShapeDtypeStruct((B, 128, nc), BF16)),
        compiler_params=_cparams(("arbitrary",)),
        name="nsa_compress",
    )(z(kc), z(vc), pe_t, w1b, w2k, w2v)


def _split_bf16(x):
    hi = x.astype(BF16)
    lo = (x - hi.astype(F32)).astype(BF16)
    return hi, lo


def _cmp_select_body(qa_ref, kcmp_ref, vcmpT_ref, ovT_ref, oc_ref, selb_ref, *, nc):
    q0 = pl.program_id(1) * Q_TILE
    jj = lax.broadcasted_iota(jnp.int32, (nc, Q_TILE), 0)
    tt = lax.broadcasted_iota(jnp.int32, (nc, Q_TILE), 1) + q0
    cmask = (jj * CMP_STRIDE + (CMP_BLOCK - 1)) <= tt

    blk = lax.broadcasted_iota(jnp.int32, (N_SLC_PAD, Q_TILE), 0)
    ts = lax.broadcasted_iota(jnp.int32, (N_SLC_PAD, Q_TILE), 1) + q0
    cur = ts >> 6
    forced = (blk == 0) | (blk == cur) | (blk == cur - 1)
    causal = blk * SLC_BLOCK <= ts
    blkf = blk.astype(F32)

    ocT = []
    for g in range(NSA_KV_HEADS):
        psum = jnp.zeros((nc, Q_TILE), F32)
        for r in range(NSA_GROUP):
            h = g * NSA_GROUP + r
            s = lax.dot_general(kcmp_ref[0, g], qa_ref[0, h], _NT, preferred_element_type=F32)
            s = jnp.where(cmask, s, -jnp.inf)
            m = jnp.max(s, axis=0, keepdims=True)
            m = jnp.where(m > -jnp.inf, m, 0.0)
            e = jnp.where(cmask, jnp.exp(s - m), 0.0)
            den = jnp.sum(e, axis=0, keepdims=True)
            p = e / jnp.maximum(den, 1e-30)
            ocT.append(jnp.dot(vcmpT_ref[0, g * 64:(g + 1) * 64, :], p.astype(BF16),
                               preferred_element_type=F32))
            psum = psum + p
        hi, lo = _split_bf16(psum)
        imp = (jnp.dot(ovT_ref[...], hi, preferred_element_type=F32)
               + jnp.dot(ovT_ref[...], lo, preferred_element_type=F32))
        imp = jnp.where(forced, FORCE_SCORE, imp)
        imp = jnp.where(causal, imp, -1.0)
        sel = jnp.zeros((N_SLC_PAD, Q_TILE), F32)
        for _ in range(SLC_TOPK):
            mx = jnp.max(imp, axis=0, keepdims=True)
            first = jnp.min(jnp.where(imp == mx, blkf, float(N_SLC_PAD)), axis=0, keepdims=True)
            hit = blkf == first
            sel = jnp.where(hit, 1.0, sel)
            imp = jnp.where(hit, -2.0, imp)
        bias = jnp.where(sel > 0.5, 0.0, NEG)
        selb_ref[0, g] = bias.T.astype(BF16)
    for pr in range(NSA_HEADS // 2):
        pair = jnp.concatenate([ocT[2 * pr], ocT[2 * pr + 1]], axis=0)
        oc_ref[0, :, pr * LANES:(pr + 1) * LANES] = pair.T


def _overlap_T(nc):
    j = np.arange(nc)[None, :] * CMP_STRIDE
    s = np.arange(N_SLC_PAD)[:, None] * SLC_BLOCK
    ov = (j < s + SLC_BLOCK) & (j + CMP_BLOCK > s) & (np.arange(nc)[None, :] < nc - 1)
    return jnp.asarray(ov.astype(np.float32)).astype(BF16)


def _cmp_select(qa, kcmp, vcmpT):
    B, _, T, _ = qa.shape
    nc = T // CMP_STRIDE
    return pl.pallas_call(
        functools.partial(_cmp_select_body, nc=nc),
        grid=(B, T // Q_TILE),
        in_specs=[pl.BlockSpec((1, 6, Q_TILE, LANES), lambda b, i: (b, 0, i, 0)),
                  pl.BlockSpec((1, 2, nc, LANES), lambda b, i: (b, 0, 0, 0)),
                  pl.BlockSpec((1, LANES, nc), lambda b, i: (b, 0, 0)),
                  pl.BlockSpec((N_SLC_PAD, nc), lambda b, i: (0, 0))],
        out_specs=(pl.BlockSpec((1, Q_TILE, NSA_W), lambda b, i: (b, i, 0)),
                   pl.BlockSpec((1, 2, Q_TILE, N_SLC_PAD), lambda b, i: (b, 0, i, 0))),
        out_shape=(jax.ShapeDtypeStruct((B, T, NSA_W), F32),
                   jax.ShapeDtypeStruct((B, 2, T, N_SLC_PAD), BF16)),
        compiler_params=_cparams(("arbitrary", "arbitrary")),
        name="nsa_cmp_select",
    )(qa, kcmp, vcmpT, _overlap_T(nc))


def _flash_step(qrows, kt, vt, m_ref, acc_ref, mask=None):
    s = lax.dot_general(qrows, kt, _NT, preferred_element_type=F32)
    if mask is not None:
        s = jnp.where(mask, s, NEG)
    m_old = m_ref[...]
    m_new = jnp.maximum(m_old, jnp.max(s, axis=1, keepdims=True))
    alpha = jnp.exp(m_old - m_new)
    p = jnp.exp(s - m_new)
    acc_ref[...] = alpha * acc_ref[...] + jnp.dot(p.astype(BF16), vt, preferred_element_type=F32)
    m_ref[...] = m_new


def _store_heads(acc_list, o_ref):
    lane = lax.broadcasted_iota(jnp.int32, (Q_TILE, LANES), 1)
    for pr in range(len(acc_list) // 2):
        a, b = acc_list[2 * pr], acc_list[2 * pr + 1]
        lo = a / pltpu.roll(a, HEAD_DIM, 1)
        hi = pltpu.roll(b, HEAD_DIM, 1) / b
        o_ref[0, :, pr * LANES:(pr + 1) * LANES] = jnp.where(lane < HEAD_DIM, lo, hi)


def _slc_body(qa_ref, selb_ref, kaug_ref, vsp_ref, o_ref, m_ref, acc_ref, on_ref):
    i = pl.program_id(1)
    q0 = i * Q_TILE
    rows = NSA_GROUP * Q_TILE
    nfull = (q0 // KV_TILE)
    for g in range(NSA_KV_HEADS):
        bias = selb_ref[0, g]
        qrows = jnp.concatenate(
            [jnp.concatenate([bias, qa_ref[0, g * NSA_GROUP + r]], axis=1) for r in range(NSA_GROUP)],
            axis=0)
        m_ref[...] = jnp.full((rows, 1), -jnp.inf, F32)
        acc_ref[...] = jnp.zeros((rows, LANES), F32)

        def body(j, carry):
            st = pl.multiple_of(j * KV_TILE, KV_TILE)
            _flash_step(qrows, kaug_ref[0, g, pl.ds(st, KV_TILE), :], vsp_ref[0, g, pl.ds(st, KV_TILE), :],
                        m_ref, acc_ref)
            return carry

        lax.fori_loop(0, nfull, body, 0)
        st = pl.multiple_of(nfull * KV_TILE, KV_TILE)
        kpos = lax.broadcasted_iota(jnp.int32, (rows, KV_TILE), 1) + st
        tq = (lax.broadcasted_iota(jnp.int32, (rows, KV_TILE), 0) & (Q_TILE - 1)) + q0
        _flash_step(qrows, kaug_ref[0, g, pl.ds(st, KV_TILE), :], vsp_ref[0, g, pl.ds(st, KV_TILE), :],
                    m_ref, acc_ref, mask=kpos <= tq)
        for r in range(NSA_GROUP):
            on_ref[g * NSA_GROUP + r] = acc_ref[r * Q_TILE:(r + 1) * Q_TILE, :]
    _store_heads([on_ref[h] for h in range(NSA_HEADS)], o_ref)


def _slc_attn(qa, selb, kaug, vsp):
    B, _, T, _ = qa.shape
    assert T % KV_TILE == 0
    rows = NSA_GROUP * Q_TILE
    return pl.pallas_call(
        _slc_body,
        grid=(B, T // Q_TILE),
        in_specs=[pl.BlockSpec((1, 6, Q_TILE, LANES), lambda b, i: (b, 0, i, 0)),
                  pl.BlockSpec((1, 2, Q_TILE, N_SLC_PAD), lambda b, i: (b, 0, i, 0)),
                  pl.BlockSpec((1, 2, T, 2 * LANES), lambda b, i: (b, 0, 0, 0)),
                  pl.BlockSpec((1, 2, T, LANES), lambda b, i: (b, 0, 0, 0))],
        out_specs=pl.BlockSpec((1, Q_TILE, NSA_W), lambda b, i: (b, i, 0)),
        out_shape=jax.ShapeDtypeStruct((B, T, NSA_W), F32),
        scratch_shapes=[pltpu.VMEM((rows, 1), F32), pltpu.VMEM((rows, LANES), F32),
                        pltpu.VMEM((NSA_HEADS, Q_TILE, LANES), F32)],
        compiler_params=_cparams(("arbitrary", "arbitrary")),
        name="nsa_slc_attn",
    )(qa, selb, kaug, vsp)


def _win_body(qa_ref, kwp_ref, vwp_ref, o_ref, m_ref, acc_ref, on_ref):
    i = pl.program_id(1)
    q0 = i * Q_TILE
    rows = NSA_GROUP * Q_TILE
    ntile = WIN // Q_TILE + 1
    cc = lax.broadcasted_iota(jnp.int32, (rows, Q_TILE), 1)
    tq = (lax.broadcasted_iota(jnp.int32, (rows, Q_TILE), 0) & (Q_TILE - 1)) + q0
    for g in range(NSA_KV_HEADS):
        qrows = jnp.concatenate([qa_ref[0, g * NSA_GROUP + r] for r in range(NSA_GROUP)], axis=0)
        m_ref[...] = jnp.full((rows, 1), -jnp.inf, F32)
        acc_ref[...] = jnp.zeros((rows, LANES), F32)
        for d in range(ntile):
            jt = i - (ntile - 1) + d
            st = pl.multiple_of(jnp.maximum(jt, 0) * Q_TILE, Q_TILE)
            kt = kwp_ref[0, g, pl.ds(st, Q_TILE), :]
            vt = vwp_ref[0, g, pl.ds(st, Q_TILE), :]
            kpos = cc + jt * Q_TILE
            if d == 0:
                mask = (kpos >= 0) & ((tq - kpos) < WIN)
            elif d == ntile - 1:
                mask = kpos <= tq
            else:
                mask = kpos >= 0
            _flash_step(qrows, kt, vt, m_ref, acc_ref, mask=mask)
        for r in range(NSA_GROUP):
            on_ref[g * NSA_GROUP + r] = acc_ref[r * Q_TILE:(r + 1) * Q_TILE, :]
    _store_heads([on_ref[h] for h in range(NSA_HEADS)], o_ref)


def _win_attn(qa, kwp, vwp):
    B, _, T, _ = qa.shape
    rows = NSA_GROUP * Q_TILE
    return pl.pallas_call(
        _win_body,
        grid=(B, T // Q_TILE),
        in_specs=[pl.BlockSpec((1, 6, Q_TILE, LANES), lambda b, i: (b, 0, i, 0)),
                  pl.BlockSpec((1, 2, T, LANES), lambda b, i: (b, 0, 0, 0)),
                  pl.BlockSpec((1, 2, T, LANES), lambda b, i: (b, 0, 0, 0))],
        out_specs=pl.BlockSpec((1, Q_TILE, NSA_W), lambda b, i: (b, i, 0)),
        out_shape=jax.ShapeDtypeStruct((B, T, NSA_W), F32),
        scratch_shapes=[pltpu.VMEM((rows, 1), F32), pltpu.VMEM((rows, LANES), F32),
                        pltpu.VMEM((NSA_HEADS, Q_TILE, LANES), F32)],
        compiler_params=_cparams(("arbitrary", "arbitrary")),
        name="nsa_win_attn",
    )(qa, kwp, vwp)


def _dil_body(q_ref, kp_ref, kc_ref, vp_ref, vc_ref, ol_ref, *, nsub, nseq):
    i = pl.program_id(1)
    a = lax.broadcasted_iota(jnp.int32, (DIL_BLOCK, 2 * DIL_BLOCK), 0)
    c = lax.broadcasted_iota(jnp.int32, (DIL_BLOCK, 2 * DIL_BLOCK), 1)
    dist = a + DIL_BLOCK - c
    band = (dist >= 0) & (dist <= DIL_BLOCK)
    lane = lax.broadcasted_iota(jnp.int32, (DIL_BLOCK, LANES), 1)

    def seq(s, carry):
        for sub in range(nsub):
            lo, hi = sub * DIL_BLOCK, (sub + 1) * DIL_BLOCK
            q = q_ref[0, s, lo:hi, :]
            if sub == 0:
                kprev, vprev = kp_ref[0, s], vp_ref[0, s]
            else:
                kprev, vprev = kc_ref[0, s, lo - DIL_BLOCK:lo, :], vc_ref[0, s, lo - DIL_BLOCK:lo, :]
            mask = band & ((c + (i * nsub + sub - 1) * DIL_BLOCK) >= 0)
            kk = jnp.concatenate([kprev, kc_ref[0, s, lo:hi, :]], axis=0)
            vv = jnp.concatenate([vprev, vc_ref[0, s, lo:hi, :]], axis=0)
            sc = lax.dot_general(q, kk, _NT, preferred_element_type=F32)
            sc = jnp.where(mask, sc, NEG)
            m = jnp.max(sc, axis=1, keepdims=True)
            e = jnp.exp(sc - m)
            pv = jnp.dot(e.astype(BF16), vv, preferred_element_type=F32)
            ol_ref[0, s, lo:hi, :] = jnp.where(lane < HEAD_DIM, pv / pltpu.roll(pv, HEAD_DIM, 1),
                                               m + jnp.log(pv))
        return carry

    lax.fori_loop(0, nseq, seq, 0)


def _dil_attn(q, k, v, nsub):
    B, S, L, _ = q.shape
    qb = nsub * DIL_BLOCK
    assert L % qb == 0
    cur = pl.BlockSpec((1, S, qb, LANES), lambda b, i: (b, 0, i, 0))
    prev = pl.BlockSpec((1, S, DIL_BLOCK, LANES), lambda b, i: (b, 0, jnp.maximum(i * nsub - 1, 0), 0))
    return pl.pallas_call(
        functools.partial(_dil_body, nsub=nsub, nseq=S),
        grid=(B, L // qb),
        in_specs=[cur, prev, cur, prev, cur],
        out_specs=cur,
        out_shape=jax.ShapeDtypeStruct((B, S, L, LANES), F32),
        compiler_params=_cparams(("arbitrary", "arbitrary")),
        name="dilated_attn",
    )(q, k, k, v, v)


def _dilated(dq, dk, dv):
    B, T, _ = dq.shape
    outs, lses = [], []
    for gi, (_, dil) in enumerate(DIL_PAIRS):
        L = T // dil
        S = 2 * dil

        def sub(a, ones):
            a = a[:, :, gi * LANES:(gi + 1) * LANES].reshape(B, L, dil, 2, HEAD_DIM)
            a = a.transpose(0, 2, 3, 1, 4).reshape(B, S, L, HEAD_DIM)
            pad = jnp.ones_like(a) if ones else jnp.zeros_like(a)
            return jnp.concatenate([a, pad], axis=-1)

        nsub = max(1, min(4, L // DIL_BLOCK // 4))
        ol = _dil_attn(sub(dq, False), sub(dk, False), sub(dv, True), nsub)

        def unsub(a):
            return a.reshape(B, dil, 2, L, HEAD_DIM).transpose(0, 3, 1, 2, 4).reshape(B, T, LANES)

        outs.append(unsub(ol[..., :HEAD_DIM]))
        lses.append(unsub(ol[..., HEAD_DIM:]))
    return outs, lses


def _gate_expand():
    m = np.zeros((3, LANES, NSA_W), np.float32)
    for j in range(3):
        for h in range(NSA_HEADS):
            m[j, h * 3 + j, h * HEAD_DIM:(h + 1) * HEAD_DIM] = 1.0
    return jnp.asarray(m).astype(BF16)


def _out_body(h_ref, conv_ref, halo_ref, oc_ref, os_ref, ow_ref, ng_ref, nz_ref,
              o0_ref, o1_ref, o2_ref, l0_ref, l1_ref, l2_ref, dz_ref, p_ref,
              wout_ref, wgate_ref, wproj_ref, cw_ref, cb_ref, gexp_ref, nf_ref, out_ref, *, tm, final):
    i = pl.program_id(1)
    cv = conv_ref[0]
    a_b, a_c, a_h, a_z = cv[:, 0:256], cv[:, 256:512], cv[:, 512:768], cv[:, 768:1024]
    u = a_c * a_h
    hl = halo_ref[0]
    uh = hl[:, 256:512] * hl[:, 512:768] * (i > 0).astype(F32)
    rowi = lax.broadcasted_iota(jnp.int32, (tm, CONV_W), 0)
    u1 = jnp.where(rowi == 0, uh[7:8, :], pltpu.roll(u, 1, 0))
    u2 = jnp.where(rowi == 0, uh[6:7, :], jnp.where(rowi == 1, uh[7:8, :], pltpu.roll(u, 2, 0)))
    cw = cw_ref[...]
    conv = cw[0:1, :] * u2 + cw[1:2, :] * u1 + cw[2:3, :] * u + cb_ref[...]
    y_a = a_b * conv * jax.nn.silu(a_z)
    gates = jax.nn.sigmoid(ng_ref[0])
    ghi, glo = _split_bf16(gates)
    y_b = jnp.zeros((tm, NSA_W), F32)
    for j, o_ref in enumerate((oc_ref, os_ref, ow_ref)):
        gj = (jnp.dot(ghi, gexp_ref[j], preferred_element_type=F32)
              + jnp.dot(glo, gexp_ref[j], preferred_element_type=F32))
        y_b = y_b + gj * o_ref[0]
    y_b = y_b * jax.nn.silu(nz_ref[0])
    l0, l1, l2 = l0_ref[0], l1_ref[0], l2_ref[0]
    mx = jnp.maximum(jnp.maximum(l0, l1), l2)
    e0, e1, e2 = jnp.exp(l0 - mx), jnp.exp(l1 - mx), jnp.exp(l2 - mx)
    den = e0 + e1 + e2
    y_c = jnp.concatenate([(e0 / den) * o0_ref[0], (e1 / den) * o1_ref[0], (e2 / den) * o2_ref[0]], axis=1)
    y_c = y_c * jax.nn.silu(dz_ref[0])
    y = jnp.concatenate([y_a, y_b, y_c], axis=1).astype(BF16)
    h1 = h_ref[0] + jnp.dot(y, wout_ref[...], preferred_element_type=F32)
    gate = jax.nn.sigmoid(jnp.dot(h1.astype(BF16), wgate_ref[...], preferred_element_type=F32))
    pp = jnp.dot(p_ref[0].astype(BF16), wproj_ref[...], preferred_element_type=F32)
    h2 = h1 + gate * pp
    if final:
        ms = jnp.mean(h2 * h2, axis=-1, keepdims=True)
        h2 = h2 * lax.rsqrt(ms + RMS_EPS) * nf_ref[...]
    out_ref[0] = h2


def _out_proj(h, conv, oc, os_, ow, ng, nz, outs, lses, dz, p, wout, wgate, wproj, cw, cb, nf, final):
    B, T, D = h.shape
    tm = min(256, T)
    row = lambda n: pl.BlockSpec((1, tm, n), lambda b, i: (b, i, 0))
    full = lambda shape: pl.BlockSpec(shape, lambda b, i: (0,) * len(shape))
    halo = pl.BlockSpec((1, 8, 1024), lambda b, i: (b, jnp.maximum(i * (tm // 8) - 1, 0), 0))
    in_specs = [row(D), row(1024), halo, row(384), row(384), row(384), row(128), row(384),
                row(128), row(128), row(128), row(128), row(128), row(128), row(384), row(PLE_DIM),
                full((D, D)), full((D, D)), full((PLE_DIM, D)), full((8, CONV_W)), full((1, CONV_W)),
                full((3, LANES, NSA_W)), full((1, D))]
    return pl.pallas_call(
        functools.partial(_out_body, tm=tm, final=final),
        grid=(B, T // tm),
        in_specs=in_specs,
        out_specs=row(D),
        out_shape=jax.ShapeDtypeStruct((B, T, D), F32),
        compiler_params=_cparams(("arbitrary", "arbitrary")),
        name="out_proj",
    )(h, conv, conv, oc, os_, ow, ng, nz, *outs, *lses, dz, p, wout, wgate, wproj, cw, cb,
      _gate_expand(), nf)


def kernel(x, p, norm_mix, w_in, conv_w, conv_b, cmp_pe, cmp_w1, cmp_w2, w_out, w_ple_gate, w_ple_proj,
           norm_final):
    B, T, D = x.shape
    depth = w_in.shape[0]
    assert D == D_MODEL and T % 2048 == 0 and T // SLC_BLOCK <= N_SLC_PAD
    h = x
    for i in range(depth):
        w = _permute_w_in(w_in[i])
        (conv, qa, kaug, vsp, kwp, vwp, kc, vc, ng, nz, dq, dk, dv, dz) = _inproj(
            h, norm_mix[i].reshape(1, D), w)
        kcmp, vcmpT = _compress(kc, vc, *_compress_weights(cmp_pe[i], cmp_w1[i], cmp_w2[i]))
        oc, selb = _cmp_select(qa, kcmp, vcmpT)
        os_ = _slc_attn(qa, selb, kaug, vsp)
        ow = _win_attn(qa, kwp, vwp)
        outs, lses = _dilated(dq, dk, dv)
        cw = jnp.zeros((8, CONV_W), F32).at[0:CONV_K].set(conv_w[i])
        h = _out_proj(h, conv, oc, os_, ow, ng, nz, outs, lses, dz, p[i],
                      w_out[i].astype(BF16), w_ple_gate[i].astype(BF16), w_ple_proj[i].astype(BF16),
                      cw, conv_b[i].reshape(1, CONV_W), norm_final.reshape(1, D), final=(i == depth - 1))
    return h
```

```python
import functools

import numpy as np
import jax
import jax.numpy as jnp
from jax import lax
from jax.experimental import pallas as pl
from jax.experimental.pallas import tpu as pltpu

F32 = jnp.float32
BF16 = jnp.bfloat16

D_MODEL = 1024
PLE_DIM = 256
HEAD_DIM = 64
RMS_EPS = 1e-6
CONV_W = 256
CONV_K = 3
NSA_HEADS = 6
NSA_KV_HEADS = 2
NSA_GROUP = 3
NSA_W = 384
CMP_BLOCK = 32
CMP_STRIDE = 16
SLC_BLOCK = 64
SLC_TOPK = 16
WIN = 512
FORCE_SCORE = 1e4
DIL_PAIRS = ((128, 1), (512, 4), (2048, 16))
DIL_W = 384
DIL_BLOCK = 128
IN_SIZES = (256,) * 4 + (384,) + (128,) * 6 + (18, 384) + (384,) * 4

LANES = 128
N_SLC_PAD = 128
Q_TILE = 128
SLC_Q_TILE = 256
V_ROWS = 80
KV_TILE = 256
ROW_TILE = 512
NEG = -1e30
SCALE = HEAD_DIM ** -0.5
VMEM_LIMIT = 56 * 1024 * 1024

_NT = (((1,), (1,)), ((), ()))


def _cparams(sem):
    return pltpu.CompilerParams(dimension_semantics=sem, vmem_limit_bytes=VMEM_LIMIT)


_C_CONV = 0
_C_QA = 1024
_C_KS = _C_QA + 768
_C_VS = _C_KS + 256
_C_KW = _C_VS + 256
_C_VW = _C_KW + 256
_C_KC = _C_VW + 256
_C_VC = _C_KC + 128
_C_NG = _C_VC + 128
_C_NZ = _C_NG + 128
_C_DQ = _C_NZ + 384
_C_DK = _C_DQ + 384
_C_DV = _C_DK + 384
_C_DZ = _C_DV + 384
_C_END = _C_DZ + 384


def _permute_w_in(w):
    offs = np.concatenate([[0], np.cumsum(IN_SIZES)])
    seg = lambda k: w[:, int(offs[k]):int(offs[k + 1])]
    z64 = jnp.zeros((w.shape[0], HEAD_DIM), w.dtype)

    def pad_heads(a, n, scale=None):
        cols = []
        for h in range(n):
            c = a[:, h * HEAD_DIM:(h + 1) * HEAD_DIM]
            cols += [c * scale if scale is not None else c, z64]
        return cols

    cols = [seg(0), seg(1), seg(2), seg(3)]
    cols += pad_heads(seg(4), NSA_HEADS, SCALE)
    cols += pad_heads(seg(7), 2) + pad_heads(seg(8), 2)
    cols += pad_heads(seg(9), 2) + pad_heads(seg(10), 2)
    cols += [seg(5), seg(6)]
    cols += [seg(11), jnp.zeros((w.shape[0], LANES - 18), w.dtype)]
    cols += [seg(12), seg(13) * SCALE, seg(14), seg(15), seg(16)]
    out = jnp.concatenate(cols, axis=1)
    assert out.shape[1] == _C_END
    return out.astype(BF16)


def _inproj_body(x_ref, g_ref, w_ref, conv_ref, qa_ref, kaug_ref, vspT_ref, kwp_ref, vwp_ref,
                 kc_ref, vc_ref, ng_ref, nz_ref, dq_ref, dk_ref, dv_ref, dz_ref, *, tm):
    x = x_ref[0]
    ms = jnp.mean(x * x, axis=-1, keepdims=True)
    xn = (x * lax.rsqrt(ms + RMS_EPS) * g_ref[...]).astype(BF16)

    def seg(a, b):
        return jnp.dot(xn, w_ref[:, a:b], preferred_element_type=F32)

    conv_ref[0] = seg(_C_CONV, _C_QA)
    q = seg(_C_QA, _C_KS)
    for h in range(NSA_HEADS):
        qa_ref[0, h] = q[:, h * LANES:(h + 1) * LANES].astype(BF16)

    t0 = pl.program_id(1) * tm
    row = lax.broadcasted_iota(jnp.int32, (tm, LANES), 0) + t0
    lane = lax.broadcasted_iota(jnp.int32, (tm, LANES), 1)
    onehot = jnp.where((row >> 6) == lane, 1.0, 0.0).astype(BF16)
    ones_hi = lane >= HEAD_DIM

    kv = seg(_C_KS, _C_KC)
    for g in range(NSA_KV_HEADS):
        ks = kv[:, g * LANES:(g + 1) * LANES].astype(BF16)
        kaug_ref[0, g] = jnp.concatenate([onehot, ks], axis=1)
        vs = kv[:, 256 + g * LANES:256 + (g + 1) * LANES]
        vspT_ref[0, g] = jnp.where(ones_hi, 1.0, vs).T[0:V_ROWS, :].astype(BF16)
        kwp_ref[0, g] = kv[:, 512 + g * LANES:512 + (g + 1) * LANES].astype(BF16)
        vw = kv[:, 768 + g * LANES:768 + (g + 1) * LANES]
        vwp_ref[0, g] = jnp.where(ones_hi, 1.0, vw).astype(BF16)

    misc = seg(_C_KC, _C_DQ)
    kc_ref[0] = misc[:, 0:128]
    vc_ref[0] = misc[:, 128:256]
    ng_ref[0] = misc[:, 256:384]
    nz_ref[0] = misc[:, 384:768]
    dil = seg(_C_DQ, _C_END)
    dq_ref[0] = dil[:, 0:384]
    dk_ref[0] = dil[:, 384:768]
    dv_ref[0] = dil[:, 768:1152]
    dz_ref[0] = dil[:, 1152:1536]


def _inproj(h, g, w):
    B, T, D = h.shape
    tm = min(ROW_TILE, T)
    assert T % tm == 0
    row = lambda n: pl.BlockSpec((1, tm, n), lambda b, i: (b, i, 0))
    head = lambda nh, n: pl.BlockSpec((1, nh, tm, n), lambda b, i: (b, 0, i, 0))
    sd = jax.ShapeDtypeStruct
    out_shape = (
        sd((B, T, 1024), F32), sd((B, 6, T, LANES), BF16), sd((B, 2, T, 2 * LANES), BF16),
        sd((B, 2, V_ROWS, T), BF16), sd((B, 2, T, LANES), BF16), sd((B, 2, T, LANES), BF16),
        sd((B, T, 128), F32), sd((B, T, 128), F32), sd((B, T, 128), F32), sd((B, T, 384), F32),
        sd((B, T, 384), F32), sd((B, T, 384), F32), sd((B, T, 384), F32), sd((B, T, 384), F32))
    headT = pl.BlockSpec((1, 2, V_ROWS, tm), lambda b, i: (b, 0, 0, i))
    out_specs = (row(1024), head(6, LANES), head(2, 2 * LANES), headT, head(2, LANES),
                 head(2, LANES), row(128), row(128), row(128), row(384),
                 row(384), row(384), row(384), row(384))
    return pl.pallas_call(
        functools.partial(_inproj_body, tm=tm),
        grid=(B, T // tm),
        in_specs=[row(D), pl.BlockSpec((1, D), lambda b, i: (0, 0)),
                  pl.BlockSpec((D, _C_END), lambda b, i: (0, 0))],
        out_specs=out_specs, out_shape=out_shape,
        compiler_params=_cparams(("arbitrary", "arbitrary")),
        name="inproj",
    )(h, g, w)


def _compress_weights(pe, w1, w2):
    eye = jnp.eye(NSA_KV_HEADS, dtype=F32)
    pe4 = pe.reshape(2, 2, 16, 1, HEAD_DIM)
    pe_t = jnp.broadcast_to(pe4, (2, 2, 16, NSA_KV_HEADS, HEAD_DIM)).reshape(2, 2, 16, 1, 128)
    w1r = w1.reshape(2, 2, 16, HEAD_DIM, HEAD_DIM)
    w1b = jnp.einsum("khmeo,gf->khmgefo", w1r, eye).reshape(2, 2, 16, 128, 128)
    w2k = jnp.zeros((NSA_KV_HEADS, 128, 128), F32)
    for g in range(NSA_KV_HEADS):
        w2k = w2k.at[g, g * 64:(g + 1) * 64, 0:64].set(w2[0])
    w2v = jnp.einsum("eo,gf->gefo", w2[1], eye).reshape(128, 128)
    return pe_t.astype(F32), w1b.astype(BF16), w2k.astype(BF16), w2v.astype(BF16)


def _compress_body(kc_ref, vc_ref, pe_ref, w1_ref, w2k_ref, w2v_ref, kcmp_ref, vcmpT_ref, *, nc):
    def hidden(x_ref, kv):
        a0 = jnp.zeros((nc, LANES), F32)
        a1 = jnp.zeros((nc, LANES), F32)
        for m in range(CMP_STRIDE):
            z = x_ref[0, pl.ds(m, nc, stride=CMP_STRIDE), :]
            a0 = a0 + jnp.dot((z + pe_ref[kv, 0, m]).astype(BF16), w1_ref[kv, 0, m],
                              preferred_element_type=F32)
            a1 = a1 + jnp.dot((z + pe_ref[kv, 1, m]).astype(BF16), w1_ref[kv, 1, m],
                              preferred_element_type=F32)
        hid = a0 + pltpu.roll(a1, nc - 1, 0)
        return jax.nn.silu(hid).astype(BF16)

    hk = hidden(kc_ref, 0)
    for g in range(NSA_KV_HEADS):
        kcmp_ref[0, g] = jnp.dot(hk, w2k_ref[g], preferred_element_type=F32).astype(BF16)
    hv = hidden(vc_ref, 1)
    vout = jnp.dot(hv, w2v_ref[...], preferred_element_type=F32)
    vcmpT_ref[0] = vout.T.astype(BF16)


def _compress(kc, vc, pe_t, w1b, w2k, w2v):
    B, T, _ = kc.shape
    nc = T // CMP_STRIDE
    full = lambda shape: pl.BlockSpec(shape, lambda b: (0,) * len(shape))
    return pl.pallas_call(
        functools.partial(_compress_body, nc=nc),
        grid=(B,),
        in_specs=[pl.BlockSpec((1, T, LANES), lambda b: (b, 0, 0)),
                  pl.BlockSpec((1, T, LANES), lambda b: (b, 0, 0)),
                  full((2, 2, 16, 1, 128)), full((2, 2, 16, 128, 128)), full((2, 128, 128)), full((128, 128))],
        out_specs=(pl.BlockSpec((1, 2, nc, 128), lambda b: (b, 0, 0, 0)),
                   pl.BlockSpec((1, 128, nc), lambda b: (b, 0, 0))),
        out_shape=(jax.ShapeDtypeStruct((B, 2, nc, 128), BF16),
                   jax.ShapeDtypeStruct((B, 128, nc), BF16)),
        compiler_params=_cparams(("arbitrary",)),
        name="nsa_compress",
    )(kc, vc, pe_t, w1b, w2k, w2v)


def _split_bf16(x):
    hi = x.astype(BF16)
    lo = (x - hi.astype(F32)).astype(BF16)
    return hi, lo


def _cmp_select_body(qa_ref, kcmp_ref, vcmpT_ref, ovT_ref, oc_ref, selb_ref, *, nc):
    q0 = pl.program_id(1) * Q_TILE
    jj = lax.broadcasted_iota(jnp.int32, (nc, Q_TILE), 0)
    tt = lax.broadcasted_iota(jnp.int32, (nc, Q_TILE), 1) + q0
    cmask = (jj * CMP_STRIDE + (CMP_BLOCK - 1)) <= tt

    blk = lax.broadcasted_iota(jnp.int32, (N_SLC_PAD, Q_TILE), 0)
    ts = lax.broadcasted_iota(jnp.int32, (N_SLC_PAD, Q_TILE), 1) + q0
    cur = ts >> 6
    forced = (blk == 0) | (blk == cur) | (blk == cur - 1)
    causal = blk * SLC_BLOCK <= ts
    blkf = blk.astype(F32)

    ocT = []
    for g in range(NSA_KV_HEADS):
        psum = jnp.zeros((nc, Q_TILE), F32)
        for r in range(NSA_GROUP):
            h = g * NSA_GROUP + r
            s = lax.dot_general(kcmp_ref[0, g], qa_ref[0, h], _NT, preferred_element_type=F32)
            s = jnp.where(cmask, s, -jnp.inf)
            m = jnp.max(s, axis=0, keepdims=True)
            m = jnp.where(m > -jnp.inf, m, 0.0)
            e = jnp.where(cmask, jnp.exp(s - m), 0.0)
            den = jnp.sum(e, axis=0, keepdims=True)
            p = e / jnp.maximum(den, 1e-30)
            ocT.append(jnp.dot(vcmpT_ref[0, g * 64:(g + 1) * 64, :], p.astype(BF16),
                               preferred_element_type=F32))
            psum = psum + p
        hi, lo = _split_bf16(psum)
        imp = (jnp.dot(ovT_ref[...], hi, preferred_element_type=F32)
               + jnp.dot(ovT_ref[...], lo, preferred_element_type=F32))
        imp = jnp.where(forced, FORCE_SCORE, imp)
        imp = jnp.where(causal, imp, -1.0)
        sel = jnp.zeros((N_SLC_PAD, Q_TILE), F32)
        for _ in range(SLC_TOPK):
            mx = jnp.max(imp, axis=0, keepdims=True)
            first = jnp.min(jnp.where(imp == mx, blkf, float(N_SLC_PAD)), axis=0, keepdims=True)
            hit = blkf == first
            sel = jnp.where(hit, 1.0, sel)
            imp = jnp.where(hit, -2.0, imp)
        bias = jnp.where(sel > 0.5, 0.0, NEG)
        selb_ref[0, g] = bias.T.astype(BF16)
    for pr in range(NSA_HEADS // 2):
        pair = jnp.concatenate([ocT[2 * pr], ocT[2 * pr + 1]], axis=0)
        oc_ref[0, :, pr * LANES:(pr + 1) * LANES] = pair.T


def _overlap_T(nc):
    j = np.arange(nc)[None, :] * CMP_STRIDE
    s = np.arange(N_SLC_PAD)[:, None] * SLC_BLOCK
    ov = (j < s + SLC_BLOCK) & (j + CMP_BLOCK > s) & (np.arange(nc)[None, :] < nc - 1)
    return jnp.asarray(ov.astype(np.float32)).astype(BF16)


def _cmp_select(qa, kcmp, vcmpT):
    B, _, T, _ = qa.shape
    nc = T // CMP_STRIDE
    return pl.pallas_call(
        functools.partial(_cmp_select_body, nc=nc),
        grid=(B, T // Q_TILE),
        in_specs=[pl.BlockSpec((1, 6, Q_TILE, LANES), lambda b, i: (b, 0, i, 0)),
                  pl.BlockSpec((1, 2, nc, LANES), lambda b, i: (b, 0, 0, 0)),
                  pl.BlockSpec((1, LANES, nc), lambda b, i: (b, 0, 0)),
                  pl.BlockSpec((N_SLC_PAD, nc), lambda b, i: (0, 0))],
        out_specs=(pl.BlockSpec((1, Q_TILE, NSA_W), lambda b, i: (b, i, 0)),
                   pl.BlockSpec((1, 2, Q_TILE, N_SLC_PAD), lambda b, i: (b, 0, i, 0))),
        out_shape=(jax.ShapeDtypeStruct((B, T, NSA_W), F32),
                   jax.ShapeDtypeStruct((B, 2, T, N_SLC_PAD), BF16)),
        compiler_params=_cparams(("arbitrary", "arbitrary")),
        name="nsa_cmp_select",
    )(qa, kcmp, vcmpT, _overlap_T(nc))


def _flash_step(qrows, kt, vt, m_ref, acc_ref, mask=None):
    s = lax.dot_general(qrows, kt, _NT, preferred_element_type=F32)
    if mask is not None:
        s = jnp.where(mask, s, NEG)
    m_old = m_ref[...]
    m_new = jnp.maximum(m_old, jnp.max(s, axis=1, keepdims=True))
    alpha = jnp.exp(m_old - m_new)
    p = jnp.exp(s - m_new)
    acc_ref[...] = alpha * acc_ref[...] + jnp.dot(p.astype(BF16), vt, preferred_element_type=F32)
    m_ref[...] = m_new


def _store_heads(acc_list, o_ref):
    lane = lax.broadcasted_iota(jnp.int32, (Q_TILE, LANES), 1)
    for pr in range(len(acc_list) // 2):
        a, b = acc_list[2 * pr], acc_list[2 * pr + 1]
        lo = a / pltpu.roll(a, HEAD_DIM, 1)
        hi = pltpu.roll(b, HEAD_DIM, 1) / b
        o_ref[0, :, pr * LANES:(pr + 1) * LANES] = jnp.where(lane < HEAD_DIM, lo, hi)


M_INIT = -1e29


def _store_heads_T(acc_ref, o_ref, qs):
    for sub in range(qs // Q_TILE):
        outs = []
        for g in range(NSA_KV_HEADS):
            for r in range(NSA_GROUP):
                lo = r * qs + sub * Q_TILE
                a = acc_ref[g, :, lo:lo + Q_TILE]
                outs.append(a[0:HEAD_DIM, :] / a[HEAD_DIM:HEAD_DIM + 1, :])
        for pr in range(NSA_HEADS // 2):
            pair = jnp.concatenate([outs[2 * pr], outs[2 * pr + 1]], axis=0)
            o_ref[0, sub * Q_TILE:(sub + 1) * Q_TILE, pr * LANES:(pr + 1) * LANES] = pair.T


def _slc_body(qa_ref, selb_ref, kaug_ref, vspT_ref, o_ref, acc_ref, s_ref, p_ref, al_ref, *, qs):
    i = pl.program_id(1)
    q0 = i * qs
    rows = NSA_GROUP * qs
    nfull = q0 // KV_TILE
    qrows = []
    for g in range(NSA_KV_HEADS):
        bias = selb_ref[0, g]
        qrows.append(jnp.concatenate(
            [jnp.concatenate([bias, qa_ref[0, g * NSA_GROUP + r]], axis=1) for r in range(NSA_GROUP)],
            axis=0))
    acc_ref[...] = jnp.zeros(acc_ref.shape, F32)
    s_ref[...] = jnp.full(s_ref.shape, NEG, F32)
    p_ref[...] = jnp.zeros(p_ref.shape, BF16)
    al_ref[...] = jnp.ones(al_ref.shape, F32)
    kpos = lax.broadcasted_iota(jnp.int32, (KV_TILE, rows), 0)
    tq = (lax.broadcasted_iota(jnp.int32, (KV_TILE, rows), 1) & (qs - 1)) + q0

    def trip(j, ms, qk, softmax):
        st_c = pl.multiple_of(jnp.maximum(j - 2, 0) * KV_TILE, KV_TILE)
        st_a = pl.multiple_of(j * KV_TILE, KV_TILE)
        out = []
        for g in range(NSA_KV_HEADS):
            acc_ref[g] = al_ref[g] * acc_ref[g] + jnp.dot(
                vspT_ref[0, g, :, pl.ds(st_c, KV_TILE)], p_ref[g], preferred_element_type=F32)
            m_new = ms[g]
            if softmax:
                s = s_ref[g]
                m_new = jnp.maximum(ms[g], jnp.max(s, axis=0, keepdims=True))
                al_ref[g] = jnp.exp(ms[g] - m_new)
                p_ref[g] = jnp.exp(s - m_new).astype(BF16)
            out.append(m_new)
            if qk is not None:
                s = lax.dot_general(kaug_ref[0, g, pl.ds(st_a, KV_TILE), :], qrows[g], _NT,
                                    preferred_element_type=F32)
                if qk == "masked":
                    s = jnp.where(kpos + st_a <= tq, s, NEG)
                s_ref[g] = s
        return tuple(out)

    ms = tuple(jnp.full((1, rows), M_INIT, F32) for _ in range(NSA_KV_HEADS))
    ms = lax.fori_loop(0, nfull, lambda j, ms: trip(j, ms, "plain", True), ms)
    ms = trip(nfull, ms, "masked", True)
    ms = trip(nfull + 1, ms, None, True)
    trip(nfull + 2, ms, None, False)
    _store_heads_T(acc_ref, o_ref, qs)


def _slc_attn(qa, selb, kaug, vspT, qs=SLC_Q_TILE):
    B, _, T, _ = qa.shape
    assert T % KV_TILE == 0 and KV_TILE % qs == 0
    rows = NSA_GROUP * qs
    return pl.pallas_call(
        functools.partial(_slc_body, qs=qs),
        grid=(B, T // qs),
        in_specs=[pl.BlockSpec((1, 6, qs, LANES), lambda b, i: (b, 0, i, 0)),
                  pl.BlockSpec((1, 2, qs, N_SLC_PAD), lambda b, i: (b, 0, i, 0)),
                  pl.BlockSpec((1, 2, T, 2 * LANES), lambda b, i: (b, 0, 0, 0)),
                  pl.BlockSpec((1, 2, V_ROWS, T), lambda b, i: (b, 0, 0, 0))],
        out_specs=pl.BlockSpec((1, qs, NSA_W), lambda b, i: (b, i, 0)),
        out_shape=jax.ShapeDtypeStruct((B, T, NSA_W), F32),
        scratch_shapes=[pltpu.VMEM((NSA_KV_HEADS, V_ROWS, rows), F32),
                        pltpu.VMEM((NSA_KV_HEADS, KV_TILE, rows), F32),
                        pltpu.VMEM((NSA_KV_HEADS, KV_TILE, rows), BF16),
                        pltpu.VMEM((NSA_KV_HEADS, 1, rows), F32)],
        compiler_params=_cparams(("arbitrary", "arbitrary")),
        name="nsa_slc_attn",
    )(qa, selb, kaug, vspT)


def _win_body(qa_ref, kwp_ref, vwp_ref, o_ref, m_ref, acc_ref, on_ref):
    i = pl.program_id(1)
    q0 = i * Q_TILE
    rows = NSA_GROUP * Q_TILE
    ntile = WIN // Q_TILE + 1
    cc = lax.broadcasted_iota(jnp.int32, (rows, Q_TILE), 1)
    tq = (lax.broadcasted_iota(jnp.int32, (rows, Q_TILE), 0) & (Q_TILE - 1)) + q0
    for g in range(NSA_KV_HEADS):
        qrows = jnp.concatenate([qa_ref[0, g * NSA_GROUP + r] for r in range(NSA_GROUP)], axis=0)
        m_ref[...] = jnp.full((rows, 1), -jnp.inf, F32)
        acc_ref[...] = jnp.zeros((rows, LANES), F32)
        for d in range(ntile):
            jt = i - (ntile - 1) + d
            st = pl.multiple_of(jnp.maximum(jt, 0) * Q_TILE, Q_TILE)
            kt = kwp_ref[0, g, pl.ds(st, Q_TILE), :]
            vt = vwp_ref[0, g, pl.ds(st, Q_TILE), :]
            kpos = cc + jt * Q_TILE
            if d == 0:
                mask = (kpos >= 0) & ((tq - kpos) < WIN)
            elif d == ntile - 1:
                mask = kpos <= tq
            else:
                mask = kpos >= 0
            _flash_step(qrows, kt, vt, m_ref, acc_ref, mask=mask)
        for r in range(NSA_GROUP):
            on_ref[g * NSA_GROUP + r] = acc_ref[r * Q_TILE:(r + 1) * Q_TILE, :]
    _store_heads([on_ref[h] for h in range(NSA_HEADS)], o_ref)


def _win_attn(qa, kwp, vwp):
    B, _, T, _ = qa.shape
    rows = NSA_GROUP * Q_TILE
    return pl.pallas_call(
        _win_body,
        grid=(B, T // Q_TILE),
        in_specs=[pl.BlockSpec((1, 6, Q_TILE, LANES), lambda b, i: (b, 0, i, 0)),
                  pl.BlockSpec((1, 2, T, LANES), lambda b, i: (b, 0, 0, 0)),
                  pl.BlockSpec((1, 2, T, LANES), lambda b, i: (b, 0, 0, 0))],
        out_specs=pl.BlockSpec((1, Q_TILE, NSA_W), lambda b, i: (b, i, 0)),
        out_shape=jax.ShapeDtypeStruct((B, T, NSA_W), F32),
        scratch_shapes=[pltpu.VMEM((rows, 1), F32), pltpu.VMEM((rows, LANES), F32),
                        pltpu.VMEM((NSA_HEADS, Q_TILE, LANES), F32)],
        compiler_params=_cparams(("arbitrary", "arbitrary")),
        name="nsa_win_attn",
    )(qa, kwp, vwp)


DIL_CHUNK = 2048
DIL_UNROLL = 4


def _dil_body(q_ref, kp_ref, kc_ref, vp_ref, vc_ref, o_ref, l_ref, kbuf, vbuf, *, dil):
    i = pl.program_id(1)
    span = DIL_BLOCK * dil
    nsb = DIL_CHUNK // span
    kbuf[0:span, :] = kp_ref[0]
    kbuf[span:span + DIL_CHUNK, :] = kc_ref[0]
    vbuf[0:span, :] = vp_ref[0]
    vbuf[span:span + DIL_CHUNK, :] = vc_ref[0]
    a = lax.broadcasted_iota(jnp.int32, (DIL_BLOCK, 2 * DIL_BLOCK), 0)
    c = lax.broadcasted_iota(jnp.int32, (DIL_BLOCK, 2 * DIL_BLOCK), 1)
    dist = a + DIL_BLOCK - c
    band = (dist >= 0) & (dist <= DIL_BLOCK)
    head0 = lax.broadcasted_iota(jnp.int32, (DIL_BLOCK, LANES), 1) < HEAD_DIM
    shift = dil.bit_length() - 1

    def trip(t, carry):
        for k in range(DIL_UNROLL):
            u = t * DIL_UNROLL + k
            sb = u >> shift
            r = u & (dil - 1)
            row0 = sb * span + r
            if dil == 1:
                rows_q = pl.ds(row0, DIL_BLOCK)
                rows_kv = pl.ds(row0, 2 * DIL_BLOCK)
            else:
                rows_q = pl.ds(row0, DIL_BLOCK, stride=dil)
                rows_kv = pl.ds(row0, 2 * DIL_BLOCK, stride=dil)
            q = q_ref[0, rows_q, :]
            kk = kbuf[rows_kv, :].astype(BF16)
            vv = vbuf[rows_kv, :].astype(BF16)
            mask = band & ((c + (i * nsb + sb - 1) * DIL_BLOCK) >= 0)
            o_h, l_h = [], []
            for hsel in (head0, jnp.logical_not(head0)):
                qh = jnp.where(hsel, q, 0.0).astype(BF16)
                sc = lax.dot_general(qh, kk, _NT, preferred_element_type=F32)
                sc = jnp.where(mask, sc, NEG)
                m = jnp.max(sc, axis=1, keepdims=True)
                e = jnp.exp(sc - m)
                den = jnp.sum(e, axis=1, keepdims=True)
                pv = jnp.dot(e.astype(BF16), vv, preferred_element_type=F32)
                o_h.append(pv / den)
                l_h.append(jnp.broadcast_to(m + jnp.log(den), (DIL_BLOCK, LANES)))
            o_ref[0, rows_q, :] = jnp.where(head0, o_h[0], o_h[1])
            l_ref[0, rows_q, :] = jnp.where(head0, l_h[0], l_h[1])
        return carry

    lax.fori_loop(0, DIL_CHUNK // DIL_BLOCK // DIL_UNROLL, trip, 0)


def _dil_attn(dq, dk, dv, gi, dil):
    B, T, _ = dq.shape
    span = DIL_BLOCK * dil
    assert T % DIL_CHUNK == 0 and DIL_CHUNK % span == 0
    cur = pl.BlockSpec((1, DIL_CHUNK, LANES), lambda b, i: (b, i, gi))
    prev = pl.BlockSpec((1, span, LANES), lambda b, i: (b, jnp.maximum(i * (DIL_CHUNK // span) - 1, 0), gi))
    out = pl.BlockSpec((1, DIL_CHUNK, LANES), lambda b, i: (b, i, 0))
    return pl.pallas_call(
        functools.partial(_dil_body, dil=dil),
        grid=(B, T // DIL_CHUNK),
        in_specs=[cur, prev, cur, prev, cur],
        out_specs=(out, out),
        out_shape=(jax.ShapeDtypeStruct((B, T, LANES), F32), jax.ShapeDtypeStruct((B, T, LANES), F32)),
        scratch_shapes=[pltpu.VMEM((span + DIL_CHUNK, LANES), F32), pltpu.VMEM((span + DIL_CHUNK, LANES), F32)],
        compiler_params=_cparams(("arbitrary", "arbitrary")),
        name="dilated_attn",
    )(dq, dk, dk, dv, dv)


def _dilated(dq, dk, dv):
    outs, lses = [], []
    for gi, (window, dil) in enumerate(DIL_PAIRS):
        assert window == DIL_BLOCK * dil
        o, l = _dil_attn(dq, dk, dv, gi, dil)
        outs.append(o)
        lses.append(l)
    return outs, lses


def _gate_expand():
    m = np.zeros((3, LANES, NSA_W), np.float32)
    for j in range(3):
        for h in range(NSA_HEADS):
            m[j, h * 3 + j, h * HEAD_DIM:(h + 1) * HEAD_DIM] = 1.0
    return jnp.asarray(m).astype(BF16)


def _out_body(h_ref, conv_ref, halo_ref, oc_ref, os_ref, ow_ref, ng_ref, nz_ref,
              o0_ref, o1_ref, o2_ref, l0_ref, l1_ref, l2_ref, dz_ref, p_ref,
              wout_ref, wgate_ref, wproj_ref, cw_ref, cb_ref, gexp_ref, nf_ref, out_ref, *, tm, final):
    i = pl.program_id(1)
    cv = conv_ref[0]
    a_b, a_c, a_h, a_z = cv[:, 0:256], cv[:, 256:512], cv[:, 512:768], cv[:, 768:1024]
    u = a_c * a_h
    hl = halo_ref[0]
    uh = hl[:, 256:512] * hl[:, 512:768] * (i > 0).astype(F32)
    rowi = lax.broadcasted_iota(jnp.int32, (tm, CONV_W), 0)
    u1 = jnp.where(rowi == 0, uh[7:8, :], pltpu.roll(u, 1, 0))
    u2 = jnp.where(rowi == 0, uh[6:7, :], jnp.where(rowi == 1, uh[7:8, :], pltpu.roll(u, 2, 0)))
    cw = cw_ref[...]
    conv = cw[0:1, :] * u2 + cw[1:2, :] * u1 + cw[2:3, :] * u + cb_ref[...]
    y_a = a_b * conv * jax.nn.silu(a_z)
    gates = jax.nn.sigmoid(ng_ref[0])
    ghi, glo = _split_bf16(gates)
    y_b = jnp.zeros((tm, NSA_W), F32)
    for j, o_ref in enumerate((oc_ref, os_ref, ow_ref)):
        gj = (jnp.dot(ghi, gexp_ref[j], preferred_element_type=F32)
              + jnp.dot(glo, gexp_ref[j], preferred_element_type=F32))
        y_b = y_b + gj * o_ref[0]
    y_b = y_b * jax.nn.silu(nz_ref[0])
    l0, l1, l2 = l0_ref[0], l1_ref[0], l2_ref[0]
    mx = jnp.maximum(jnp.maximum(l0, l1), l2)
    e0, e1, e2 = jnp.exp(l0 - mx), jnp.exp(l1 - mx), jnp.exp(l2 - mx)
    den = e0 + e1 + e2
    y_c = jnp.concatenate([(e0 / den) * o0_ref[0], (e1 / den) * o1_ref[0], (e2 / den) * o2_ref[0]], axis=1)
    y_c = y_c * jax.nn.silu(dz_ref[0])
    y = jnp.concatenate([y_a, y_b, y_c], axis=1).astype(BF16)
    h1 = h_ref[0] + jnp.dot(y, wout_ref[...], preferred_element_type=F32)
    gate = jax.nn.sigmoid(jnp.dot(h1.astype(BF16), wgate_ref[...], preferred_element_type=F32))
    pp = jnp.dot(p_ref[0].astype(BF16), wproj_ref[...], preferred_element_type=F32)
    h2 = h1 + gate * pp
    if final:
        ms = jnp.mean(h2 * h2, axis=-1, keepdims=True)
        h2 = h2 * lax.rsqrt(ms + RMS_EPS) * nf_ref[...]
    out_ref[0] = h2


def _out_proj(h, conv, oc, os_, ow, ng, nz, outs, lses, dz, p, wout, wgate, wproj, cw, cb, nf, final):
    B, T, D = h.shape
    tm = min(256, T)
    row = lambda n: pl.BlockSpec((1, tm, n), lambda b, i: (b, i, 0))
    full = lambda shape: pl.BlockSpec(shape, lambda b, i: (0,) * len(shape))
    halo = pl.BlockSpec((1, 8, 1024), lambda b, i: (b, jnp.maximum(i * (tm // 8) - 1, 0), 0))
    in_specs = [row(D), row(1024), halo, row(384), row(384), row(384), row(128), row(384),
                row(128), row(128), row(128), row(128), row(128), row(128), row(384), row(PLE_DIM),
                full((D, D)), full((D, D)), full((PLE_DIM, D)), full((8, CONV_W)), full((1, CONV_W)),
                full((3, LANES, NSA_W)), full((1, D))]
    return pl.pallas_call(
        functools.partial(_out_body, tm=tm, final=final),
        grid=(B, T // tm),
        in_specs=in_specs,
        out_specs=row(D),
        out_shape=jax.ShapeDtypeStruct((B, T, D), F32),
        compiler_params=_cparams(("arbitrary", "arbitrary")),
        name="out_proj",
    )(h, conv, conv, oc, os_, ow, ng, nz, *outs, *lses, dz, p, wout, wgate, wproj, cw, cb,
      _gate_expand(), nf)


def kernel(x, p, norm_mix, w_in, conv_w, conv_b, cmp_pe, cmp_w1, cmp_w2, w_out, w_ple_gate, w_ple_proj,
           norm_final):
    B, T, D = x.shape
    depth = w_in.shape[0]
    assert D == D_MODEL and T % 2048 == 0 and T // SLC_BLOCK <= N_SLC_PAD
    h = x
    for i in range(depth):
        w = _permute_w_in(w_in[i])
        (conv, qa, kaug, vspT, kwp, vwp, kc, vc, ng, nz, dq, dk, dv, dz) = _inproj(
            h, norm_mix[i].reshape(1, D), w)
        kcmp, vcmpT = _compress(kc, vc, *_compress_weights(cmp_pe[i], cmp_w1[i], cmp_w2[i]))
        oc, selb = _cmp_select(qa, kcmp, vcmpT)
        os_ = _slc_attn(qa, selb, kaug, vspT)
        ow = _win_attn(qa, kwp, vwp)
        outs, lses = _dilated(dq, dk, dv)
        cw = jnp.zeros((8, CONV_W), F32).at[0:CONV_K].set(conv_w[i])
        h = _out_proj(h, conv, oc, os_, ow, ng, nz, outs, lses, dz, p[i],
                      w_out[i].astype(BF16), w_ple_gate[i].astype(BF16), w_ple_proj[i].astype(BF16),
                      cw, conv_b[i].reshape(1, CONV_W), norm_final.reshape(1, D), final=(i == depth - 1))
    return h
```

```python
import functools

import numpy as np
import jax
import jax.numpy as jnp
from jax import lax
from jax.experimental import pallas as pl
from jax.experimental.pallas import tpu as pltpu

F32 = jnp.float32
BF16 = jnp.bfloat16

D_MODEL = 1024
PLE_DIM = 256
HEAD_DIM = 64
RMS_EPS = 1e-6
CONV_W = 256
CONV_K = 3
NSA_HEADS = 6
NSA_KV_HEADS = 2
NSA_GROUP = 3
NSA_W = 384
CMP_BLOCK = 32
CMP_STRIDE = 16
SLC_BLOCK = 64
SLC_TOPK = 16
WIN = 512
FORCE_SCORE = 1e4
DIL_PAIRS = ((128, 1), (512, 4), (2048, 16))
DIL_W = 384
DIL_BLOCK = 128
IN_SIZES = (256,) * 4 + (384,) + (128,) * 6 + (18, 384) + (384,) * 4

LANES = 128
N_SLC_PAD = 128
Q_TILE = 128
SLC_Q_TILE = 256
CMP_Q_TILE = 256
WIN_Q_TILE = 256
V_ROWS = 80
KV_TILE = 256
ROW_TILE = 512
NEG = -1e30
SCALE = HEAD_DIM ** -0.5
LOG2E = float(np.log2(np.e))
VMEM_LIMIT = 56 * 1024 * 1024

_NT = (((1,), (1,)), ((), ()))


def _cparams(sem):
    return pltpu.CompilerParams(dimension_semantics=sem, vmem_limit_bytes=VMEM_LIMIT)


_C_CONV = 0
_C_QA = 1024
_C_KS = _C_QA + 768
_C_VS = _C_KS + 256
_C_KW = _C_VS + 256
_C_VW = _C_KW + 256
_C_KC = _C_VW + 256
_C_VC = _C_KC + 128
_C_NG = _C_VC + 128
_C_NZ = _C_NG + 128
_C_DQ = _C_NZ + 384
_C_DK = _C_DQ + 384
_C_DV = _C_DK + 384
_C_DZ = _C_DV + 384
_C_END = _C_DZ + 384


def _permute_w_in(w):
    offs = np.concatenate([[0], np.cumsum(IN_SIZES)])
    seg = lambda k: w[:, int(offs[k]):int(offs[k + 1])]
    z64 = jnp.zeros((w.shape[0], HEAD_DIM), w.dtype)

    def pad_heads(a, n, scale=None):
        cols = []
        for h in range(n):
            c = a[:, h * HEAD_DIM:(h + 1) * HEAD_DIM]
            cols += [c * scale if scale is not None else c, z64]
        return cols

    cols = [seg(0), seg(1), seg(2), seg(3)]
    cols += pad_heads(seg(4), NSA_HEADS, SCALE * LOG2E)
    cols += pad_heads(seg(7), 2) + pad_heads(seg(8), 2)
    cols += pad_heads(seg(9), 2) + pad_heads(seg(10), 2)
    cols += [seg(5), seg(6)]
    cols += [seg(11), jnp.zeros((w.shape[0], LANES - 18), w.dtype)]
    cols += [seg(12), seg(13) * SCALE, seg(14), seg(15), seg(16)]
    out = jnp.concatenate(cols, axis=1)
    assert out.shape[1] == _C_END
    return out.astype(BF16)


def _inproj_body(x_ref, g_ref, w_ref, conv_ref, qa_ref, kaug_ref, vspT_ref, kwp_ref, vwT_ref,
                 kc_ref, vc_ref, ng_ref, nz_ref, dq_ref, dk_ref, dv_ref, dz_ref, *, tm):
    x = x_ref[0]
    ms = jnp.mean(x * x, axis=-1, keepdims=True)
    xn = (x * lax.rsqrt(ms + RMS_EPS) * g_ref[...]).astype(BF16)

    def seg(a, b):
        return jnp.dot(xn, w_ref[:, a:b], preferred_element_type=F32)

    conv_ref[0] = seg(_C_CONV, _C_QA)
    q = seg(_C_QA, _C_KS)
    for h in range(NSA_HEADS):
        qa_ref[0, h] = q[:, h * LANES:(h + 1) * LANES].astype(BF16)

    t0 = pl.program_id(1) * tm
    row = lax.broadcasted_iota(jnp.int32, (tm, LANES), 0) + t0
    lane = lax.broadcasted_iota(jnp.int32, (tm, LANES), 1)
    onehot = jnp.where((row >> 6) == lane, 1.0, 0.0).astype(BF16)
    ones_hi = lane >= HEAD_DIM

    kv = seg(_C_KS, _C_KC)
    for g in range(NSA_KV_HEADS):
        ks = kv[:, g * LANES:(g + 1) * LANES].astype(BF16)
        kaug_ref[0, g] = jnp.concatenate([onehot, ks], axis=1)
        vs = kv[:, 256 + g * LANES:256 + (g + 1) * LANES]
        vspT_ref[0, g] = jnp.where(ones_hi, 1.0, vs).T[0:V_ROWS, :].astype(BF16)
        kwp_ref[0, g] = kv[:, 512 + g * LANES:512 + (g + 1) * LANES].astype(BF16)
        vw = kv[:, 768 + g * LANES:768 + (g + 1) * LANES]
        vwT_ref[0, g] = jnp.where(ones_hi, 1.0, vw).T[0:V_ROWS, :].astype(BF16)

    misc = seg(_C_KC, _C_DQ)
    kc_ref[0] = misc[:, 0:128]
    vc_ref[0] = misc[:, 128:256]
    ng_ref[0] = misc[:, 256:384]
    nz_ref[0] = misc[:, 384:768]
    dil = seg(_C_DQ, _C_END)
    dq_ref[0] = dil[:, 0:384]
    dk_ref[0] = dil[:, 384:768]
    dv_ref[0] = dil[:, 768:1152]
    dz_ref[0] = dil[:, 1152:1536]


def _inproj(h, g, w):
    B, T, D = h.shape
    tm = min(ROW_TILE, T)
    assert T % tm == 0
    row = lambda n: pl.BlockSpec((1, tm, n), lambda b, i: (b, i, 0))
    head = lambda nh, n: pl.BlockSpec((1, nh, tm, n), lambda b, i: (b, 0, i, 0))
    sd = jax.ShapeDtypeStruct
    out_shape = (
        sd((B, T, 1024), F32), sd((B, 6, T, LANES), BF16), sd((B, 2, T, 2 * LANES), BF16),
        sd((B, 2, V_ROWS, T), BF16), sd((B, 2, T, LANES), BF16), sd((B, 2, V_ROWS, T), BF16),
        sd((B, T, 128), F32), sd((B, T, 128), F32), sd((B, T, 128), F32), sd((B, T, 384), F32),
        sd((B, T, 384), F32), sd((B, T, 384), F32), sd((B, T, 384), F32), sd((B, T, 384), F32))
    headT = pl.BlockSpec((1, 2, V_ROWS, tm), lambda b, i: (b, 0, 0, i))
    out_specs = (row(1024), head(6, LANES), head(2, 2 * LANES), headT, head(2, LANES),
                 headT, row(128), row(128), row(128), row(384),
                 row(384), row(384), row(384), row(384))
    return pl.pallas_call(
        functools.partial(_inproj_body, tm=tm),
        grid=(B, T // tm),
        in_specs=[row(D), pl.BlockSpec((1, D), lambda b, i: (0, 0)),
                  pl.BlockSpec((D, _C_END), lambda b, i: (0, 0))],
        out_specs=out_specs, out_shape=out_shape,
        compiler_params=_cparams(("arbitrary", "arbitrary")),
        name="inproj",
    )(h, g, w)


def _compress_weights(pe, w1, w2):
    eye = jnp.eye(NSA_KV_HEADS, dtype=F32)
    pe4 = pe.reshape(2, 2, 16, 1, HEAD_DIM)
    pe_t = jnp.broadcast_to(pe4, (2, 2, 16, NSA_KV_HEADS, HEAD_DIM)).reshape(2, 2, 16, 1, 128)
    w1r = w1.reshape(2, 2, 16, HEAD_DIM, HEAD_DIM)
    w1b = jnp.einsum("khmeo,gf->khmgefo", w1r, eye).reshape(2, 2, 16, 128, 128)
    w2k = jnp.zeros((NSA_KV_HEADS, 128, 128), F32)
    for g in range(NSA_KV_HEADS):
        w2k = w2k.at[g, g * 64:(g + 1) * 64, 0:64].set(w2[0])
    w2v = jnp.einsum("eo,gf->gefo", w2[1], eye).reshape(128, 128)
    return pe_t.astype(F32), w1b.astype(BF16), w2k.astype(BF16), w2v.astype(BF16)


def _compress_body(kc_ref, vc_ref, pe_ref, w1_ref, w2k_ref, w2v_ref, kcmp_ref, vcmpT_ref, *, nc):
    def hidden(x_ref, kv):
        a0 = jnp.zeros((nc, LANES), F32)
        a1 = jnp.zeros((nc, LANES), F32)
        for m in range(CMP_STRIDE):
            z = x_ref[0, pl.ds(m, nc, stride=CMP_STRIDE), :]
            a0 = a0 + jnp.dot((z + pe_ref[kv, 0, m]).astype(BF16), w1_ref[kv, 0, m],
                              preferred_element_type=F32)
            a1 = a1 + jnp.dot((z + pe_ref[kv, 1, m]).astype(BF16), w1_ref[kv, 1, m],
                              preferred_element_type=F32)
        hid = a0 + pltpu.roll(a1, nc - 1, 0)
        return jax.nn.silu(hid).astype(BF16)

    hk = hidden(kc_ref, 0)
    for g in range(NSA_KV_HEADS):
        kcmp_ref[0, g] = jnp.dot(hk, w2k_ref[g], preferred_element_type=F32).astype(BF16)
    hv = hidden(vc_ref, 1)
    vout = jnp.dot(hv, w2v_ref[...], preferred_element_type=F32)
    vcmpT_ref[0] = vout.T.astype(BF16)


def _compress(kc, vc, pe_t, w1b, w2k, w2v):
    B, T, _ = kc.shape
    nc = T // CMP_STRIDE
    full = lambda shape: pl.BlockSpec(shape, lambda b: (0,) * len(shape))
    return pl.pallas_call(
        functools.partial(_compress_body, nc=nc),
        grid=(B,),
        in_specs=[pl.BlockSpec((1, T, LANES), lambda b: (b, 0, 0)),
                  pl.BlockSpec((1, T, LANES), lambda b: (b, 0, 0)),
                  full((2, 2, 16, 1, 128)), full((2, 2, 16, 128, 128)), full((2, 128, 128)), full((128, 128))],
        out_specs=(pl.BlockSpec((1, 2, nc, 128), lambda b: (b, 0, 0, 0)),
                   pl.BlockSpec((1, 128, nc), lambda b: (b, 0, 0))),
        out_shape=(jax.ShapeDtypeStruct((B, 2, nc, 128), BF16),
                   jax.ShapeDtypeStruct((B, 128, nc), BF16)),
        compiler_params=_cparams(("arbitrary",)),
        name="nsa_compress",
    )(kc, vc, pe_t, w1b, w2k, w2v)


def _split_bf16(x):
    hi = x.astype(BF16)
    lo = (x - hi.astype(F32)).astype(BF16)
    return hi, lo


def _cmp_select_body(qa_ref, kcmp_ref, vcmpT_ref, ovT_ref, oc_ref, selb_ref, *, nc, qt):
    q0 = pl.program_id(1) * qt
    jj = lax.broadcasted_iota(jnp.int32, (nc, qt), 0)
    tt = lax.broadcasted_iota(jnp.int32, (nc, qt), 1) + q0
    cmask = (jj * CMP_STRIDE + (CMP_BLOCK - 1)) <= tt

    blk = lax.broadcasted_iota(jnp.int32, (N_SLC_PAD, qt), 0)
    ts = lax.broadcasted_iota(jnp.int32, (N_SLC_PAD, qt), 1) + q0
    cur = ts >> 6
    forced = (blk == 0) | (blk == cur) | (blk == cur - 1)
    causal = blk * SLC_BLOCK <= ts
    blkf = blk.astype(F32)

    ocT = []
    for g in range(NSA_KV_HEADS):
        psum = jnp.zeros((nc, qt), F32)
        for r in range(NSA_GROUP):
            h = g * NSA_GROUP + r
            s = lax.dot_general(kcmp_ref[0, g], qa_ref[0, h], _NT, preferred_element_type=F32)
            s = jnp.where(cmask, s, -jnp.inf)
            m = jnp.max(s, axis=0, keepdims=True)
            m = jnp.where(m > -jnp.inf, m, 0.0)
            e = jnp.exp2(s - m)
            den = jnp.sum(e, axis=0, keepdims=True)
            rden = 1.0 / jnp.maximum(den, 1e-30)
            ocT.append(jnp.dot(vcmpT_ref[0, g * 64:(g + 1) * 64, :], e.astype(BF16),
                               preferred_element_type=F32) * rden)
            psum = psum + e * rden
        hi, lo = _split_bf16(psum)
        imp = (jnp.dot(ovT_ref[...], hi, preferred_element_type=F32)
               + jnp.dot(ovT_ref[...], lo, preferred_element_type=F32))
        imp = jnp.where(causal, imp, -1.0)
        imp = jnp.where(forced, -2.0, imp)
        sel = jnp.where(forced, 1.0, 0.0)
        for _ in range(SLC_TOPK - 3):
            mx = jnp.max(imp, axis=0, keepdims=True)
            first = jnp.min(jnp.where(imp == mx, blkf, float(N_SLC_PAD)), axis=0, keepdims=True)
            hit = blkf == first
            sel = jnp.where(hit, 1.0, sel)
            imp = jnp.where(hit, -2.0, imp)
        bias = jnp.where(sel > 0.5, 0.0, NEG)
        for sub in range(qt // Q_TILE):
            selb_ref[0, g, sub * Q_TILE:(sub + 1) * Q_TILE, :] = (
                bias[:, sub * Q_TILE:(sub + 1) * Q_TILE].T.astype(BF16))
    for pr in range(NSA_HEADS // 2):
        pair = jnp.concatenate([ocT[2 * pr], ocT[2 * pr + 1]], axis=0)
        for sub in range(qt // Q_TILE):
            oc_ref[0, sub * Q_TILE:(sub + 1) * Q_TILE, pr * LANES:(pr + 1) * LANES] = (
                pair[:, sub * Q_TILE:(sub + 1) * Q_TILE].T)


def _overlap_T(nc):
    j = np.arange(nc)[None, :] * CMP_STRIDE
    s = np.arange(N_SLC_PAD)[:, None] * SLC_BLOCK
    ov = (j < s + SLC_BLOCK) & (j + CMP_BLOCK > s) & (np.arange(nc)[None, :] < nc - 1)
    return jnp.asarray(ov.astype(np.float32)).astype(BF16)


def _cmp_select(qa, kcmp, vcmpT):
    B, _, T, _ = qa.shape
    nc = T // CMP_STRIDE
    qt = CMP_Q_TILE
    return pl.pallas_call(
        functools.partial(_cmp_select_body, nc=nc, qt=qt),
        grid=(B, T // qt),
        in_specs=[pl.BlockSpec((1, 6, qt, LANES), lambda b, i: (b, 0, i, 0)),
                  pl.BlockSpec((1, 2, nc, LANES), lambda b, i: (b, 0, 0, 0)),
                  pl.BlockSpec((1, LANES, nc), lambda b, i: (b, 0, 0)),
                  pl.BlockSpec((N_SLC_PAD, nc), lambda b, i: (0, 0))],
        out_specs=(pl.BlockSpec((1, qt, NSA_W), lambda b, i: (b, i, 0)),
                   pl.BlockSpec((1, 2, qt, N_SLC_PAD), lambda b, i: (b, 0, i, 0))),
        out_shape=(jax.ShapeDtypeStruct((B, T, NSA_W), F32),
                   jax.ShapeDtypeStruct((B, 2, T, N_SLC_PAD), BF16)),
        compiler_params=_cparams(("arbitrary", "arbitrary")),
        name="nsa_cmp_select",
    )(qa, kcmp, vcmpT, _overlap_T(nc))


M_INIT = -1e29


def _store_heads_T(acc_ref, o_ref, qs):
    for sub in range(qs // Q_TILE):
        outs = []
        for g in range(NSA_KV_HEADS):
            for r in range(NSA_GROUP):
                lo = r * qs + sub * Q_TILE
                a = acc_ref[g, :, lo:lo + Q_TILE]
                outs.append(a[0:HEAD_DIM, :] / a[HEAD_DIM:HEAD_DIM + 1, :])
        for pr in range(NSA_HEADS // 2):
            pair = jnp.concatenate([outs[2 * pr], outs[2 * pr + 1]], axis=0)
            o_ref[0, sub * Q_TILE:(sub + 1) * Q_TILE, pr * LANES:(pr + 1) * LANES] = pair.T


def _slc_body(qa_ref, selb_ref, kaug_ref, vspT_ref, o_ref, acc_ref, s_ref, p_ref, al_ref, *, qs):
    i = pl.program_id(1)
    q0 = i * qs
    rows = NSA_GROUP * qs
    nfull = q0 // KV_TILE
    t_len = kaug_ref.shape[2]
    qrows = []
    for g in range(NSA_KV_HEADS):
        bias = selb_ref[0, g]
        qrows.append(jnp.concatenate(
            [jnp.concatenate([bias, qa_ref[0, g * NSA_GROUP + r]], axis=1) for r in range(NSA_GROUP)],
            axis=0))
    acc_ref[...] = jnp.zeros(acc_ref.shape, F32)
    s_ref[...] = jnp.full(s_ref.shape, NEG, F32)
    p_ref[...] = jnp.zeros(p_ref.shape, BF16)
    al_ref[...] = jnp.ones(al_ref.shape, F32)
    kpos = lax.broadcasted_iota(jnp.int32, (KV_TILE, rows), 0)
    tq = (lax.broadcasted_iota(jnp.int32, (KV_TILE, rows), 1) & (qs - 1)) + q0

    def trip(j, par, ms, qk, softmax):
        st_c = pl.multiple_of(jnp.maximum(j - 2, 0) * KV_TILE, KV_TILE)
        st_a = pl.multiple_of(jnp.minimum(j * KV_TILE, t_len - KV_TILE), KV_TILE)
        out = []
        for g in range(NSA_KV_HEADS):
            if qk is not None:
                s = lax.dot_general(kaug_ref[0, g, pl.ds(st_a, KV_TILE), :], qrows[g], _NT,
                                    preferred_element_type=F32)
                if qk == "masked":
                    s = jnp.where(kpos + j * KV_TILE <= tq, s, NEG)
                s_ref[par, g] = s
            acc_ref[g] = al_ref[par, g] * acc_ref[g] + jnp.dot(
                vspT_ref[0, g, :, pl.ds(st_c, KV_TILE)], p_ref[par, g], preferred_element_type=F32)
            m_new = ms[g]
            if softmax:
                s = s_ref[1 - par, g]
                m_new = jnp.maximum(ms[g], jnp.max(s, axis=0, keepdims=True))
                al_ref[1 - par, g] = jnp.exp2(ms[g] - m_new)
                p_ref[1 - par, g] = jnp.exp2(s - m_new).astype(BF16)
            out.append(m_new)
        return tuple(out)

    def pair(k, ms):
        ms = trip(2 * k, 0, ms, "plain", True)
        return trip(2 * k + 1, 1, ms, "plain", True)

    npairs = nfull // 2
    ms = tuple(jnp.full((1, rows), M_INIT, F32) for _ in range(NSA_KV_HEADS))
    ms = lax.fori_loop(0, npairs, pair, ms)
    t0 = 2 * npairs
    ms = trip(t0, 0, ms, "masked", True)
    ms = trip(t0 + 1, 1, ms, "masked", True)
    ms = trip(t0 + 2, 0, ms, None, True)
    trip(t0 + 3, 1, ms, None, False)
    _store_heads_T(acc_ref, o_ref, qs)


def _slc_attn(qa, selb, kaug, vspT, qs=SLC_Q_TILE):
    B, _, T, _ = qa.shape
    assert T % KV_TILE == 0 and KV_TILE % qs == 0
    rows = NSA_GROUP * qs
    return pl.pallas_call(
        functools.partial(_slc_body, qs=qs),
        grid=(B, T // qs),
        in_specs=[pl.BlockSpec((1, 6, qs, LANES), lambda b, i: (b, 0, i, 0)),
                  pl.BlockSpec((1, 2, qs, N_SLC_PAD), lambda b, i: (b, 0, i, 0)),
                  pl.BlockSpec((1, 2, T, 2 * LANES), lambda b, i: (b, 0, 0, 0)),
                  pl.BlockSpec((1, 2, V_ROWS, T), lambda b, i: (b, 0, 0, 0))],
        out_specs=pl.BlockSpec((1, qs, NSA_W), lambda b, i: (b, i, 0)),
        out_shape=jax.ShapeDtypeStruct((B, T, NSA_W), F32),
        scratch_shapes=[pltpu.VMEM((NSA_KV_HEADS, V_ROWS, rows), F32),
                        pltpu.VMEM((2, NSA_KV_HEADS, KV_TILE, rows), F32),
                        pltpu.VMEM((2, NSA_KV_HEADS, KV_TILE, rows), BF16),
                        pltpu.VMEM((2, NSA_KV_HEADS, 1, rows), F32)],
        compiler_params=_cparams(("arbitrary", "arbitrary")),
        name="nsa_slc_attn",
    )(qa, selb, kaug, vspT)


def _win_body(qa_ref, kwp_ref, vwT_ref, o_ref, acc_ref, *, qs):
    i = pl.program_id(1)
    q0 = i * qs
    rows = NSA_GROUP * qs
    ntile = WIN // qs + 1
    kk = lax.broadcasted_iota(jnp.int32, (qs, rows), 0)
    tq = (lax.broadcasted_iota(jnp.int32, (qs, rows), 1) & (qs - 1)) + q0
    for g in range(NSA_KV_HEADS):
        qrows = jnp.concatenate([qa_ref[0, g * NSA_GROUP + r] for r in range(NSA_GROUP)], axis=0)
        scores, starts = [], []
        for d in range(ntile):
            jt = i - (ntile - 1) + d
            st = pl.multiple_of(jnp.maximum(jt, 0) * qs, qs)
            s = lax.dot_general(kwp_ref[0, g, pl.ds(st, qs), :], qrows, _NT, preferred_element_type=F32)
            kpos = kk + jt * qs
            if d == 0:
                mask = (kpos >= 0) & ((tq - kpos) < WIN)
            elif d == ntile - 1:
                mask = kpos <= tq
            else:
                mask = kpos >= 0
            scores.append(jnp.where(mask, s, NEG))
            starts.append(st)
        m = functools.reduce(jnp.maximum, [jnp.max(s, axis=0, keepdims=True) for s in scores])
        acc = jnp.zeros((V_ROWS, rows), F32)
        for s, st in zip(scores, starts):
            acc = acc + jnp.dot(vwT_ref[0, g, :, pl.ds(st, qs)], jnp.exp2(s - m).astype(BF16),
                                preferred_element_type=F32)
        acc_ref[g] = acc
    _store_heads_T(acc_ref, o_ref, qs)


def _win_attn(qa, kwp, vwT, qs=WIN_Q_TILE):
    B, _, T, _ = qa.shape
    assert WIN % qs == 0 and T % qs == 0
    rows = NSA_GROUP * qs
    return pl.pallas_call(
        functools.partial(_win_body, qs=qs),
        grid=(B, T // qs),
        in_specs=[pl.BlockSpec((1, 6, qs, LANES), lambda b, i: (b, 0, i, 0)),
                  pl.BlockSpec((1, 2, T, LANES), lambda b, i: (b, 0, 0, 0)),
                  pl.BlockSpec((1, 2, V_ROWS, T), lambda b, i: (b, 0, 0, 0))],
        out_specs=pl.BlockSpec((1, qs, NSA_W), lambda b, i: (b, i, 0)),
        out_shape=jax.ShapeDtypeStruct((B, T, NSA_W), F32),
        scratch_shapes=[pltpu.VMEM((NSA_KV_HEADS, V_ROWS, rows), F32)],
        compiler_params=_cparams(("arbitrary", "arbitrary")),
        name="nsa_win_attn",
    )(qa, kwp, vwT)


DIL_CHUNK = 2048
DIL_UNROLL = 4


def _dil_body(q_ref, kp_ref, kc_ref, vp_ref, vc_ref, o_ref, l_ref, kbuf, vbuf, *, dil):
    i = pl.program_id(1)
    span = DIL_BLOCK * dil
    nsb = DIL_CHUNK // span
    kbuf[0:span, :] = kp_ref[0]
    kbuf[span:span + DIL_CHUNK, :] = kc_ref[0]
    vbuf[0:span, :] = vp_ref[0]
    vbuf[span:span + DIL_CHUNK, :] = vc_ref[0]
    a = lax.broadcasted_iota(jnp.int32, (DIL_BLOCK, 2 * DIL_BLOCK), 0)
    c = lax.broadcasted_iota(jnp.int32, (DIL_BLOCK, 2 * DIL_BLOCK), 1)
    dist = a + DIL_BLOCK - c
    band = (dist >= 0) & (dist <= DIL_BLOCK)
    head0 = lax.broadcasted_iota(jnp.int32, (DIL_BLOCK, LANES), 1) < HEAD_DIM
    shift = dil.bit_length() - 1

    def trip(t, carry):
        for k in range(DIL_UNROLL):
            u = t * DIL_UNROLL + k
            sb = u >> shift
            r = u & (dil - 1)
            row0 = sb * span + r
            if dil == 1:
                rows_q = pl.ds(row0, DIL_BLOCK)
                rows_kv = pl.ds(row0, 2 * DIL_BLOCK)
            else:
                rows_q = pl.ds(row0, DIL_BLOCK, stride=dil)
                rows_kv = pl.ds(row0, 2 * DIL_BLOCK, stride=dil)
            q = q_ref[0, rows_q, :]
            kk = kbuf[rows_kv, :].astype(BF16)
            vv = vbuf[rows_kv, :].astype(BF16)
            mask = band & ((c + (i * nsb + sb - 1) * DIL_BLOCK) >= 0)
            o_h, l_h = [], []
            for hsel in (head0, jnp.logical_not(head0)):
                qh = jnp.where(hsel, q, 0.0).astype(BF16)
                sc = lax.dot_general(qh, kk, _NT, preferred_element_type=F32)
                sc = jnp.where(mask, sc, NEG)
                m = jnp.max(sc, axis=1, keepdims=True)
                e = jnp.exp(sc - m)
                den = jnp.sum(e, axis=1, keepdims=True)
                pv = jnp.dot(e.astype(BF16), vv, preferred_element_type=F32)
                o_h.append(pv / den)
                l_h.append(jnp.broadcast_to(m + jnp.log(den), (DIL_BLOCK, LANES)))
            o_ref[0, rows_q, :] = jnp.where(head0, o_h[0], o_h[1])
            l_ref[0, rows_q, :] = jnp.where(head0, l_h[0], l_h[1])
        return carry

    lax.fori_loop(0, DIL_CHUNK // DIL_BLOCK // DIL_UNROLL, trip, 0)


def _dil_attn(dq, dk, dv, gi, dil):
    B, T, _ = dq.shape
    span = DIL_BLOCK * dil
    assert T % DIL_CHUNK == 0 and DIL_CHUNK % span == 0
    cur = pl.BlockSpec((1, DIL_CHUNK, LANES), lambda b, i: (b, i, gi))
    prev = pl.BlockSpec((1, span, LANES), lambda b, i: (b, jnp.maximum(i * (DIL_CHUNK // span) - 1, 0), gi))
    out = pl.BlockSpec((1, DIL_CHUNK, LANES), lambda b, i: (b, i, 0))
    return pl.pallas_call(
        functools.partial(_dil_body, dil=dil),
        grid=(B, T // DIL_CHUNK),
        in_specs=[cur, prev, cur, prev, cur],
        out_specs=(out, out),
        out_shape=(jax.ShapeDtypeStruct((B, T, LANES), F32), jax.ShapeDtypeStruct((B, T, LANES), F32)),
        scratch_shapes=[pltpu.VMEM((span + DIL_CHUNK, LANES), F32), pltpu.VMEM((span + DIL_CHUNK, LANES), F32)],
        compiler_params=_cparams(("arbitrary", "arbitrary")),
        name="dilated_attn",
    )(dq, dk, dk, dv, dv)


def _dilated(dq, dk, dv):
    outs, lses = [], []
    for gi, (window, dil) in enumerate(DIL_PAIRS):
        assert window == DIL_BLOCK * dil
        o, l = _dil_attn(dq, dk, dv, gi, dil)
        outs.append(o)
        lses.append(l)
    return outs, lses


def _gate_expand():
    m = np.zeros((3, LANES, NSA_W), np.float32)
    for j in range(3):
        for h in range(NSA_HEADS):
            m[j, h * 3 + j, h * HEAD_DIM:(h + 1) * HEAD_DIM] = 1.0
    return jnp.asarray(m).astype(BF16)


def _out_body(h_ref, conv_ref, halo_ref, oc_ref, os_ref, ow_ref, ng_ref, nz_ref,
              o0_ref, o1_ref, o2_ref, l0_ref, l1_ref, l2_ref, dz_ref, p_ref,
              wout_ref, wgate_ref, wproj_ref, cw_ref, cb_ref, gexp_ref, nf_ref, out_ref, *, tm, final):
    i = pl.program_id(1)
    cv = conv_ref[0]
    a_b, a_c, a_h, a_z = cv[:, 0:256], cv[:, 256:512], cv[:, 512:768], cv[:, 768:1024]
    u = a_c * a_h
    hl = halo_ref[0]
    uh = hl[:, 256:512] * hl[:, 512:768] * (i > 0).astype(F32)
    rowi = lax.broadcasted_iota(jnp.int32, (tm, CONV_W), 0)
    u1 = jnp.where(rowi == 0, uh[7:8, :], pltpu.roll(u, 1, 0))
    u2 = jnp.where(rowi == 0, uh[6:7, :], jnp.where(rowi == 1, uh[7:8, :], pltpu.roll(u, 2, 0)))
    cw = cw_ref[...]
    conv = cw[0:1, :] * u2 + cw[1:2, :] * u1 + cw[2:3, :] * u + cb_ref[...]
    y_a = a_b * conv * jax.nn.silu(a_z)
    gates = jax.nn.sigmoid(ng_ref[0])
    ghi, glo = _split_bf16(gates)
    y_b = jnp.zeros((tm, NSA_W), F32)
    for j, o_ref in enumerate((oc_ref, os_ref, ow_ref)):
        gj = (jnp.dot(ghi, gexp_ref[j], preferred_element_type=F32)
              + jnp.dot(glo, gexp_ref[j], preferred_element_type=F32))
        y_b = y_b + gj * o_ref[0]
    y_b = y_b * jax.nn.silu(nz_ref[0])
    l0, l1, l2 = l0_ref[0], l1_ref[0], l2_ref[0]
    mx = jnp.maximum(jnp.maximum(l0, l1), l2)
    e0, e1, e2 = jnp.exp(l0 - mx), jnp.exp(l1 - mx), jnp.exp(l2 - mx)
    den = e0 + e1 + e2
    y_c = jnp.concatenate([(e0 / den) * o0_ref[0], (e1 / den) * o1_ref[0], (e2 / den) * o2_ref[0]], axis=1)
    y_c = y_c * jax.nn.silu(dz_ref[0])
    y = jnp.concatenate([y_a, y_b, y_c], axis=1).astype(BF16)
    h1 = h_ref[0] + jnp.dot(y, wout_ref[...], preferred_element_type=F32)
    gate = jax.nn.sigmoid(jnp.dot(h1.astype(BF16), wgate_ref[...], preferred_element_type=F32))
    pp = jnp.dot(p_ref[0].astype(BF16), wproj_ref[...], preferred_element_type=F32)
    h2 = h1 + gate * pp
    if final:
        ms = jnp.mean(h2 * h2, axis=-1, keepdims=True)
        h2 = h2 * lax.rsqrt(ms + RMS_EPS) * nf_ref[...]
    out_ref[0] = h2


def _out_proj(h, conv, oc, os_, ow, ng, nz, outs, lses, dz, p, wout, wgate, wproj, cw, cb, nf, final):
    B, T, D = h.shape
    tm = min(256, T)
    row = lambda n: pl.BlockSpec((1, tm, n), lambda b, i: (b, i, 0))
    full = lambda shape: pl.BlockSpec(shape, lambda b, i: (0,) * len(shape))
    halo = pl.BlockSpec((1, 8, 1024), lambda b, i: (b, jnp.maximum(i * (tm // 8) - 1, 0), 0))
    in_specs = [row(D), row(1024), halo, row(384), row(384), row(384), row(128), row(384),
                row(128), row(128), row(128), row(128), row(128), row(128), row(384), row(PLE_DIM),
                full((D, D)), full((D, D)), full((PLE_DIM, D)), full((8, CONV_W)), full((1, CONV_W)),
                full((3, LANES, NSA_W)), full((1, D))]
    return pl.pallas_call(
        functools.partial(_out_body, tm=tm, final=final),
        grid=(B, T // tm),
        in_specs=in_specs,
        out_specs=row(D),
        out_shape=jax.ShapeDtypeStruct((B, T, D), F32),
        compiler_params=_cparams(("arbitrary", "arbitrary")),
        name="out_proj",
    )(h, conv, conv, oc, os_, ow, ng, nz, *outs, *lses, dz, p, wout, wgate, wproj, cw, cb,
      _gate_expand(), nf)


def kernel(x, p, norm_mix, w_in, conv_w, conv_b, cmp_pe, cmp_w1, cmp_w2, w_out, w_ple_gate, w_ple_proj,
           norm_final):
    B, T, D = x.shape
    depth = w_in.shape[0]
    assert D == D_MODEL and T % 2048 == 0 and T // SLC_BLOCK <= N_SLC_PAD
    h = x
    for i in range(depth):
        w = _permute_w_in(w_in[i])
        (conv, qa, kaug, vspT, kwp, vwT, kc, vc, ng, nz, dq, dk, dv, dz) = _inproj(
            h, norm_mix[i].reshape(1, D), w)
        kcmp, vcmpT = _compress(kc, vc, *_compress_weights(cmp_pe[i], cmp_w1[i], cmp_w2[i]))
        oc, selb = _cmp_select(qa, kcmp, vcmpT)
        os_ = _slc_attn(qa, selb, kaug, vspT)
        ow = _win_attn(qa, kwp, vwT)
        outs, lses = _dilated(dq, dk, dv)
        cw = jnp.zeros((8, CONV_W), F32).at[0:CONV_K].set(conv_w[i])
        h = _out_proj(h, conv, oc, os_, ow, ng, nz, outs, lses, dz, p[i],
                      w_out[i].astype(BF16), w_ple_gate[i].astype(BF16), w_ple_proj[i].astype(BF16),
                      cw, conv_b[i].reshape(1, CONV_W), norm_final.reshape(1, D), final=(i == depth - 1))
    return h
```

```python
import functools

import numpy as np
import jax
import jax.numpy as jnp
from jax import lax
from jax.experimental import pallas as pl
from jax.experimental.pallas import tpu as pltpu

F32 = jnp.float32
BF16 = jnp.bfloat16

D_MODEL = 1024
PLE_DIM = 256
HEAD_DIM = 64
RMS_EPS = 1e-6
CONV_W = 256
CONV_K = 3
NSA_HEADS = 6
NSA_KV_HEADS = 2
NSA_GROUP = 3
NSA_W = 384
CMP_BLOCK = 32
CMP_STRIDE = 16
SLC_BLOCK = 64
SLC_TOPK = 16
WIN = 512
FORCE_SCORE = 1e4
DIL_PAIRS = ((128, 1), (512, 4), (2048, 16))
DIL_W = 384
DIL_BLOCK = 128
IN_SIZES = (256,) * 4 + (384,) + (128,) * 6 + (18, 384) + (384,) * 4

LANES = 128
N_SLC_PAD = 128
Q_TILE = 128
SLC_Q_TILE = 256
CMP_Q_TILE = 256
CMP_CLASSES = 4
WIN_Q_TILE = 256
V_ROWS = 80
KV_TILE = 256
ROW_TILE = 512
NEG = -1e30
SCALE = HEAD_DIM ** -0.5
LOG2E = float(np.log2(np.e))
VMEM_LIMIT = 56 * 1024 * 1024

_NT = (((1,), (1,)), ((), ()))


def _cparams(sem):
    return pltpu.CompilerParams(dimension_semantics=sem, vmem_limit_bytes=VMEM_LIMIT)


_C_CONV = 0
_C_NQ = 1024
_C_KV = _C_NQ + 384
_C_KC = _C_KV + 512
_C_VC = _C_KC + 128
_C_NG = _C_VC + 128
_C_NZ = _C_NG + 128
_C_DQ = _C_NZ + 384
_C_END = _C_DQ + 4 * 384


def _permute_w_in(w):
    offs = np.concatenate([[0], np.cumsum(IN_SIZES)]).astype(int)
    o = lambda k: int(offs[k])
    nq = w[:, o(4):o(5)] * (SCALE * LOG2E)
    dq = w[:, o(13):o(14)] * SCALE
    pad = jnp.zeros((w.shape[0], LANES - 18), w.dtype)
    out = jnp.concatenate([w[:, o(0):o(4)], nq, w[:, o(7):o(11)], w[:, o(5):o(7)], w[:, o(11):o(12)], pad,
                           w[:, o(12):o(13)], dq, w[:, o(14):o(17)]], axis=1)
    assert out.shape[1] == _C_END
    return out.astype(BF16)


def _head_lanes(c, lane_lo, odd, fill):
    if odd:
        c = pltpu.roll(c, HEAD_DIM, 1)
    return jnp.where(lane_lo, c, fill)


def _inproj_body(x_ref, g_ref, w_ref, conv_ref, qa_ref, kaug_ref, vspT_ref, kwp_ref, vwT_ref,
                 kc_ref, vc_ref, ng_ref, nz_ref, dq_ref, dk_ref, dv_ref, dz_ref, *, tm):
    x = x_ref[0]
    ms = jnp.mean(x * x, axis=-1, keepdims=True)
    xn = (x * lax.rsqrt(ms + RMS_EPS) * g_ref[...]).astype(BF16)

    def seg(a, b):
        return jnp.dot(xn, w_ref[:, a:b], preferred_element_type=F32)

    conv_ref[0] = seg(_C_CONV, _C_NQ)

    t0 = pl.program_id(1) * tm
    row = lax.broadcasted_iota(jnp.int32, (tm, LANES), 0) + t0
    lane = lax.broadcasted_iota(jnp.int32, (tm, LANES), 1)
    onehot = jnp.where((row >> 6) == lane, 1.0, 0.0).astype(BF16)
    lane_lo = lane < HEAD_DIM

    q = seg(_C_NQ, _C_KV)
    for h in range(NSA_HEADS):
        c = q[:, (h // 2) * LANES:(h // 2 + 1) * LANES]
        qa_ref[0, h] = _head_lanes(c, lane_lo, h % 2 == 1, 0.0).astype(BF16)

    kv = seg(_C_KV, _C_KC)
    for g in range(NSA_KV_HEADS):
        ks = _head_lanes(kv[:, 0:128], lane_lo, g == 1, 0.0).astype(BF16)
        kaug_ref[0, g] = jnp.concatenate([onehot, ks], axis=1)
        vspT_ref[0, g] = _head_lanes(kv[:, 128:256], lane_lo, g == 1, 1.0).T[0:V_ROWS, :].astype(BF16)
        kwp_ref[0, g] = _head_lanes(kv[:, 256:384], lane_lo, g == 1, 0.0).astype(BF16)
        vwT_ref[0, g] = _head_lanes(kv[:, 384:512], lane_lo, g == 1, 1.0).T[0:V_ROWS, :].astype(BF16)

    misc = seg(_C_KC, _C_DQ)
    kc_ref[0] = misc[:, 0:128]
    vc_ref[0] = misc[:, 128:256]
    ng_ref[0] = misc[:, 256:384]
    nz_ref[0] = misc[:, 384:768]
    dil = seg(_C_DQ, _C_END)
    dq_ref[0] = dil[:, 0:384]
    dk_ref[0] = dil[:, 384:768]
    dv_ref[0] = dil[:, 768:1152]
    dz_ref[0] = dil[:, 1152:1536]


def _inproj(h, g, w):
    B, T, D = h.shape
    tm = min(ROW_TILE, T)
    assert T % tm == 0
    row = lambda n: pl.BlockSpec((1, tm, n), lambda b, i: (b, i, 0))
    head = lambda nh, n: pl.BlockSpec((1, nh, tm, n), lambda b, i: (b, 0, i, 0))
    sd = jax.ShapeDtypeStruct
    out_shape = (
        sd((B, T, 1024), F32), sd((B, 6, T, LANES), BF16), sd((B, 2, T, 2 * LANES), BF16),
        sd((B, 2, V_ROWS, T), BF16), sd((B, 2, T, LANES), BF16), sd((B, 2, V_ROWS, T), BF16),
        sd((B, T, 128), F32), sd((B, T, 128), F32), sd((B, T, 128), F32), sd((B, T, 384), F32),
        sd((B, T, 384), F32), sd((B, T, 384), F32), sd((B, T, 384), F32), sd((B, T, 384), F32))
    headT = pl.BlockSpec((1, 2, V_ROWS, tm), lambda b, i: (b, 0, 0, i))
    out_specs = (row(1024), head(6, LANES), head(2, 2 * LANES), headT, head(2, LANES),
                 headT, row(128), row(128), row(128), row(384),
                 row(384), row(384), row(384), row(384))
    return pl.pallas_call(
        functools.partial(_inproj_body, tm=tm),
        grid=(B, T // tm),
        in_specs=[row(D), pl.BlockSpec((1, D), lambda b, i: (0, 0)),
                  pl.BlockSpec((D, _C_END), lambda b, i: (0, 0))],
        out_specs=out_specs, out_shape=out_shape,
        compiler_params=_cparams(("arbitrary", "arbitrary")),
        name="inproj",
    )(h, g, w)


def _compress_weights(pe, w1, w2):
    eye = jnp.eye(NSA_KV_HEADS, dtype=F32)
    pe4 = pe.reshape(2, 2, 16, 1, HEAD_DIM)
    pe_t = jnp.broadcast_to(pe4, (2, 2, 16, NSA_KV_HEADS, HEAD_DIM)).reshape(2, 2, 16, 1, 128)
    w1r = w1.reshape(2, 2, 16, HEAD_DIM, HEAD_DIM)
    w1b = jnp.einsum("khmeo,gf->khmgefo", w1r, eye).reshape(2, 2, 16, 128, 128)
    w2k = jnp.zeros((NSA_KV_HEADS, 128, 128), F32)
    for g in range(NSA_KV_HEADS):
        w2k = w2k.at[g, g * 64:(g + 1) * 64, 0:64].set(w2[0])
    w2v = jnp.einsum("eo,gf->gefo", w2[1], eye).reshape(128, 128)
    return pe_t.astype(F32), w1b.astype(BF16), w2k.astype(BF16), w2v.astype(BF16)


def _compress_body(kc_ref, vc_ref, pe_ref, w1_ref, w2k_ref, w2v_ref, kcmp_ref, vcmpT_ref, *, nc):
    def hidden(x_ref, kv):
        a0 = jnp.zeros((nc, LANES), F32)
        a1 = jnp.zeros((nc, LANES), F32)
        for m in range(CMP_STRIDE):
            z = x_ref[0, pl.ds(m, nc, stride=CMP_STRIDE), :]
            a0 = a0 + jnp.dot((z + pe_ref[kv, 0, m]).astype(BF16), w1_ref[kv, 0, m],
                              preferred_element_type=F32)
            a1 = a1 + jnp.dot((z + pe_ref[kv, 1, m]).astype(BF16), w1_ref[kv, 1, m],
                              preferred_element_type=F32)
        hid = a0 + pltpu.roll(a1, nc - 1, 0)
        return jax.nn.silu(hid).astype(BF16)

    hk = hidden(kc_ref, 0)
    for g in range(NSA_KV_HEADS):
        kcmp_ref[0, g] = jnp.dot(hk, w2k_ref[g], preferred_element_type=F32).astype(BF16)
    hv = hidden(vc_ref, 1)
    vout = jnp.dot(hv, w2v_ref[...], preferred_element_type=F32)
    vcmpT_ref[0] = vout.T.astype(BF16)


def _compress(kc, vc, pe_t, w1b, w2k, w2v):
    B, T, _ = kc.shape
    nc = T // CMP_STRIDE
    full = lambda shape: pl.BlockSpec(shape, lambda b: (0,) * len(shape))
    return pl.pallas_call(
        functools.partial(_compress_body, nc=nc),
        grid=(B,),
        in_specs=[pl.BlockSpec((1, T, LANES), lambda b: (b, 0, 0)),
                  pl.BlockSpec((1, T, LANES), lambda b: (b, 0, 0)),
                  full((2, 2, 16, 1, 128)), full((2, 2, 16, 128, 128)), full((2, 128, 128)), full((128, 128))],
        out_specs=(pl.BlockSpec((1, 2, nc, 128), lambda b: (b, 0, 0, 0)),
                   pl.BlockSpec((1, 128, nc), lambda b: (b, 0, 0))),
        out_shape=(jax.ShapeDtypeStruct((B, 2, nc, 128), BF16),
                   jax.ShapeDtypeStruct((B, 128, nc), BF16)),
        compiler_params=_cparams(("arbitrary",)),
        name="nsa_compress",
    )(kc, vc, pe_t, w1b, w2k, w2v)


def _split_bf16(x):
    hi = x.astype(BF16)
    lo = (x - hi.astype(F32)).astype(BF16)
    return hi, lo


def _cmp_select_body(qa_ref, kcmp_ref, vcmpT_ref, ovT_ref, oc_ref, selb_ref, *, nc, qt, n_tiles_static):
    i = pl.program_id(1)
    q0 = i * qt

    def run(nrow, nblk):
        jj = lax.broadcasted_iota(jnp.int32, (nrow, qt), 0)
        tt = lax.broadcasted_iota(jnp.int32, (nrow, qt), 1) + q0
        cmask = (jj * CMP_STRIDE + (CMP_BLOCK - 1)) <= tt
        blk = lax.broadcasted_iota(jnp.int32, (nblk, qt), 0)
        ts = lax.broadcasted_iota(jnp.int32, (nblk, qt), 1) + q0
        cur = ts >> 6
        forced = (blk == 0) | (blk == cur) | (blk == cur - 1)
        causal = blk * SLC_BLOCK <= ts
        blkf = blk.astype(F32)
        ovT = ovT_ref[0:nblk, 0:nrow]

        raw = [lax.dot_general(kcmp_ref[0, h // NSA_GROUP, 0:nrow, :], qa_ref[0, h], _NT,
                               preferred_element_type=F32) for h in range(NSA_HEADS)]
        ocT = []
        for g in range(NSA_KV_HEADS):
            psum = jnp.zeros((nrow, qt), F32)
            for r in range(NSA_GROUP):
                h = g * NSA_GROUP + r
                s = jnp.where(cmask, raw[h], -jnp.inf)
                m = jnp.max(s, axis=0, keepdims=True)
                m = jnp.where(m > -jnp.inf, m, 0.0)
                e = jnp.exp2(s - m)
                den = jnp.sum(e, axis=0, keepdims=True)
                rden = 1.0 / jnp.maximum(den, 1e-30)
                ocT.append(jnp.dot(vcmpT_ref[0, g * 64:(g + 1) * 64, 0:nrow], e.astype(BF16),
                                   preferred_element_type=F32) * rden)
                psum = psum + e * rden
            hi, lo = _split_bf16(psum)
            imp = (jnp.dot(ovT, hi, preferred_element_type=F32)
                   + jnp.dot(ovT, lo, preferred_element_type=F32))
            imp = jnp.where(causal, imp, -1.0)
            imp = jnp.where(forced, -2.0, imp)
            sel = jnp.where(forced, 1.0, 0.0)
            for _ in range(SLC_TOPK - 3):
                mx = jnp.max(imp, axis=0, keepdims=True)
                first = jnp.min(jnp.where(imp == mx, blkf, float(N_SLC_PAD)), axis=0, keepdims=True)
                hit = blkf == first
                sel = jnp.where(hit, 1.0, sel)
                imp = jnp.where(hit, -2.0, imp)
            bias = jnp.where(sel > 0.5, 0.0, NEG)
            if nblk < N_SLC_PAD:
                bias = jnp.concatenate([bias, jnp.full((N_SLC_PAD - nblk, qt), NEG, F32)], axis=0)
            for sub in range(qt // Q_TILE):
                selb_ref[0, g, sub * Q_TILE:(sub + 1) * Q_TILE, :] = (
                    bias[:, sub * Q_TILE:(sub + 1) * Q_TILE].T.astype(BF16))
        for pr in range(NSA_HEADS // 2):
            pair = jnp.concatenate([ocT[2 * pr], ocT[2 * pr + 1]], axis=0)
            for sub in range(qt // Q_TILE):
                oc_ref[0, sub * Q_TILE:(sub + 1) * Q_TILE, pr * LANES:(pr + 1) * LANES] = (
                    pair[:, sub * Q_TILE:(sub + 1) * Q_TILE].T.astype(oc_ref.dtype))

    bounds = [(k * n_tiles_static) // CMP_CLASSES for k in range(CMP_CLASSES + 1)]
    for k in range(CMP_CLASSES):
        lo, hi = bounds[k], bounds[k + 1]
        if hi == lo:
            continue
        nrow = min(nc, -(-(hi * qt // CMP_STRIDE) // LANES) * LANES)
        nblk = min(N_SLC_PAD, max(32, -(-(hi * qt // SLC_BLOCK) // 8) * 8))
        pl.when((i >= lo) & (i < hi))(functools.partial(run, nrow, nblk))


def _overlap_T(nc):
    j = np.arange(nc)[None, :] * CMP_STRIDE
    s = np.arange(N_SLC_PAD)[:, None] * SLC_BLOCK
    ov = (j < s + SLC_BLOCK) & (j + CMP_BLOCK > s) & (np.arange(nc)[None, :] < nc - 1)
    return jnp.asarray(ov.astype(np.float32)).astype(BF16)


def _cmp_select(qa, kcmp, vcmpT):
    B, _, T, _ = qa.shape
    nc = T // CMP_STRIDE
    qt = CMP_Q_TILE
    return pl.pallas_call(
        functools.partial(_cmp_select_body, nc=nc, qt=qt, n_tiles_static=T // qt),
        grid=(B, T // qt),
        in_specs=[pl.BlockSpec((1, 6, qt, LANES), lambda b, i: (b, 0, i, 0)),
                  pl.BlockSpec((1, 2, nc, LANES), lambda b, i: (b, 0, 0, 0)),
                  pl.BlockSpec((1, LANES, nc), lambda b, i: (b, 0, 0)),
                  pl.BlockSpec((N_SLC_PAD, nc), lambda b, i: (0, 0))],
        out_specs=(pl.BlockSpec((1, qt, NSA_W), lambda b, i: (b, i, 0)),
                   pl.BlockSpec((1, 2, qt, N_SLC_PAD), lambda b, i: (b, 0, i, 0))),
        out_shape=(jax.ShapeDtypeStruct((B, T, NSA_W), BF16),
                   jax.ShapeDtypeStruct((B, 2, T, N_SLC_PAD), BF16)),
        compiler_params=_cparams(("arbitrary", "arbitrary")),
        name="nsa_cmp_select",
    )(qa, kcmp, vcmpT, _overlap_T(nc))


M_INIT = -1e29


def _store_heads_T(acc_ref, o_ref, qs):
    for sub in range(qs // Q_TILE):
        outs = []
        for g in range(NSA_KV_HEADS):
            for r in range(NSA_GROUP):
                lo = r * qs + sub * Q_TILE
                a = acc_ref[g, :, lo:lo + Q_TILE]
                outs.append(a[0:HEAD_DIM, :] / a[HEAD_DIM:HEAD_DIM + 1, :])
        for pr in range(NSA_HEADS // 2):
            pair = jnp.concatenate([outs[2 * pr], outs[2 * pr + 1]], axis=0)
            o_ref[0, sub * Q_TILE:(sub + 1) * Q_TILE, pr * LANES:(pr + 1) * LANES] = pair.T.astype(o_ref.dtype)


def _slc_body(qa_ref, selb_ref, kaug_ref, vspT_ref, o_ref, acc_ref, s_ref, p_ref, al_ref, *, qs):
    i = pl.program_id(1)
    q0 = i * qs
    rows = NSA_GROUP * qs
    nfull = q0 // KV_TILE
    t_len = kaug_ref.shape[2]
    qT = []
    for g in range(NSA_KV_HEADS):
        bias = selb_ref[0, g]
        qT.append([jnp.concatenate([bias, qa_ref[0, g * NSA_GROUP + r]], axis=1).T
                   for r in range(NSA_GROUP)])
    acc_ref[...] = jnp.zeros(acc_ref.shape, F32)
    s_ref[...] = jnp.full(s_ref.shape, NEG, F32)
    p_ref[...] = jnp.zeros(p_ref.shape, BF16)
    al_ref[...] = jnp.ones(al_ref.shape, F32)
    kpos = lax.broadcasted_iota(jnp.int32, (KV_TILE, qs), 0)
    tq = lax.broadcasted_iota(jnp.int32, (KV_TILE, qs), 1) + q0

    def trip(j, par, ms, qk, softmax):
        st_c = pl.multiple_of(jnp.maximum(j - 2, 0) * KV_TILE, KV_TILE)
        st_a = pl.multiple_of(jnp.minimum(j * KV_TILE, t_len - KV_TILE), KV_TILE)
        out = []
        for g in range(NSA_KV_HEADS):
            if qk is not None:
                kt = kaug_ref[0, g, pl.ds(st_a, KV_TILE), :]
                for r in range(NSA_GROUP):
                    s = jnp.dot(kt, qT[g][r], preferred_element_type=F32)
                    if qk == "masked":
                        s = jnp.where(kpos + j * KV_TILE <= tq, s, NEG)
                    s_ref[par, g, :, r * qs:(r + 1) * qs] = s
            vt = vspT_ref[0, g, :, pl.ds(st_c, KV_TILE)]
            for r in range(NSA_GROUP):
                cols = slice(r * qs, (r + 1) * qs)
                acc_ref[g, :, cols] = al_ref[par, g, :, cols] * acc_ref[g, :, cols] + jnp.dot(
                    vt, p_ref[par, g, :, cols], preferred_element_type=F32)
            m_new = ms[g]
            if softmax:
                s = s_ref[1 - par, g]
                m_new = jnp.maximum(ms[g], jnp.max(s, axis=0, keepdims=True))
                al_ref[1 - par, g] = jnp.exp2(ms[g] - m_new)
                p_ref[1 - par, g] = jnp.exp2(s - m_new).astype(BF16)
            out.append(m_new)
        return tuple(out)

    def pair(k, ms):
        ms = trip(2 * k, 0, ms, "plain", True)
        return trip(2 * k + 1, 1, ms, "plain", True)

    npairs = nfull // 2
    ms = tuple(jnp.full((1, rows), M_INIT, F32) for _ in range(NSA_KV_HEADS))
    ms = lax.fori_loop(0, npairs, pair, ms)
    t0 = 2 * npairs
    ms = trip(t0, 0, ms, "masked", True)
    ms = trip(t0 + 1, 1, ms, "masked", True)
    ms = trip(t0 + 2, 0, ms, None, True)
    trip(t0 + 3, 1, ms, None, False)
    _store_heads_T(acc_ref, o_ref, qs)


def _slc_attn(qa, selb, kaug, vspT, qs=SLC_Q_TILE):
    B, _, T, _ = qa.shape
    assert T % KV_TILE == 0 and KV_TILE % qs == 0
    rows = NSA_GROUP * qs
    return pl.pallas_call(
        functools.partial(_slc_body, qs=qs),
        grid=(B, T // qs),
        in_specs=[pl.BlockSpec((1, 6, qs, LANES), lambda b, i: (b, 0, i, 0)),
                  pl.BlockSpec((1, 2, qs, N_SLC_PAD), lambda b, i: (b, 0, i, 0)),
                  pl.BlockSpec((1, 2, T, 2 * LANES), lambda b, i: (b, 0, 0, 0)),
                  pl.BlockSpec((1, 2, V_ROWS, T), lambda b, i: (b, 0, 0, 0))],
        out_specs=pl.BlockSpec((1, qs, NSA_W), lambda b, i: (b, i, 0)),
        out_shape=jax.ShapeDtypeStruct((B, T, NSA_W), BF16),
        scratch_shapes=[pltpu.VMEM((NSA_KV_HEADS, V_ROWS, rows), F32),
                        pltpu.VMEM((2, NSA_KV_HEADS, KV_TILE, rows), F32),
                        pltpu.VMEM((2, NSA_KV_HEADS, KV_TILE, rows), BF16),
                        pltpu.VMEM((2, NSA_KV_HEADS, 1, rows), F32)],
        compiler_params=_cparams(("arbitrary", "arbitrary")),
        name="nsa_slc_attn",
    )(qa, selb, kaug, vspT)


def _win_body(qa_ref, kwp_ref, vwT_ref, o_ref, acc_ref, *, qs):
    i = pl.program_id(1)
    q0 = i * qs
    rows = NSA_GROUP * qs
    ntile = WIN // qs + 1
    kk = lax.broadcasted_iota(jnp.int32, (qs, rows), 0)
    tq = (lax.broadcasted_iota(jnp.int32, (qs, rows), 1) & (qs - 1)) + q0
    starts = [pl.multiple_of(jnp.maximum(i - (ntile - 1) + d, 0) * qs, qs) for d in range(ntile)]
    raw = []
    for g in range(NSA_KV_HEADS):
        qrows = jnp.concatenate([qa_ref[0, g * NSA_GROUP + r] for r in range(NSA_GROUP)], axis=0)
        raw.append([lax.dot_general(kwp_ref[0, g, pl.ds(starts[d], qs), :], qrows, _NT,
                                    preferred_element_type=F32) for d in range(ntile)])
    for g in range(NSA_KV_HEADS):
        scores = []
        for d in range(ntile):
            kpos = kk + (i - (ntile - 1) + d) * qs
            if d == 0:
                mask = (kpos >= 0) & ((tq - kpos) < WIN)
            elif d == ntile - 1:
                mask = kpos <= tq
            else:
                mask = kpos >= 0
            scores.append(jnp.where(mask, raw[g][d], NEG))
        m = functools.reduce(jnp.maximum, [jnp.max(s, axis=0, keepdims=True) for s in scores])
        acc = jnp.zeros((V_ROWS, rows), F32)
        for s, st in zip(scores, starts):
            acc = acc + jnp.dot(vwT_ref[0, g, :, pl.ds(st, qs)], jnp.exp2(s - m).astype(BF16),
                                preferred_element_type=F32)
        acc_ref[g] = acc
    _store_heads_T(acc_ref, o_ref, qs)


def _win_attn(qa, kwp, vwT, qs=WIN_Q_TILE):
    B, _, T, _ = qa.shape
    assert WIN % qs == 0 and T % qs == 0
    rows = NSA_GROUP * qs
    return pl.pallas_call(
        functools.partial(_win_body, qs=qs),
        grid=(B, T // qs),
        in_specs=[pl.BlockSpec((1, 6, qs, LANES), lambda b, i: (b, 0, i, 0)),
                  pl.BlockSpec((1, 2, T, LANES), lambda b, i: (b, 0, 0, 0)),
                  pl.BlockSpec((1, 2, V_ROWS, T), lambda b, i: (b, 0, 0, 0))],
        out_specs=pl.BlockSpec((1, qs, NSA_W), lambda b, i: (b, i, 0)),
        out_shape=jax.ShapeDtypeStruct((B, T, NSA_W), BF16),
        scratch_shapes=[pltpu.VMEM((NSA_KV_HEADS, V_ROWS, rows), F32)],
        compiler_params=_cparams(("arbitrary", "arbitrary")),
        name="nsa_win_attn",
    )(qa, kwp, vwT)


DIL_CHUNK = 2048
DIL_UNROLL = 4


def _dil_body(q_ref, kp_ref, kc_ref, vp_ref, vc_ref, o_ref, l_ref, kbuf, vbuf, *, dil):
    i = pl.program_id(1)
    span = DIL_BLOCK * dil
    nsb = DIL_CHUNK // span
    kbuf[0:span, :] = kp_ref[0]
    kbuf[span:span + DIL_CHUNK, :] = kc_ref[0]
    vbuf[0:span, :] = vp_ref[0]
    vbuf[span:span + DIL_CHUNK, :] = vc_ref[0]
    a = lax.broadcasted_iota(jnp.int32, (DIL_BLOCK, 2 * DIL_BLOCK), 0)
    c = lax.broadcasted_iota(jnp.int32, (DIL_BLOCK, 2 * DIL_BLOCK), 1)
    dist = a + DIL_BLOCK - c
    band = (dist >= 0) & (dist <= DIL_BLOCK)
    head0 = lax.broadcasted_iota(jnp.int32, (DIL_BLOCK, LANES), 1) < HEAD_DIM
    shift = dil.bit_length() - 1

    def trip(t, carry):
        work = []
        for k in range(DIL_UNROLL):
            u = t * DIL_UNROLL + k
            sb = u >> shift
            r = u & (dil - 1)
            row0 = sb * span + r
            if dil == 1:
                rows_q = pl.ds(row0, DIL_BLOCK)
                rows_kv = pl.ds(row0, 2 * DIL_BLOCK)
            else:
                rows_q = pl.ds(row0, DIL_BLOCK, stride=dil)
                rows_kv = pl.ds(row0, 2 * DIL_BLOCK, stride=dil)
            q = q_ref[0, rows_q, :]
            kk = kbuf[rows_kv, :].astype(BF16)
            scs = [lax.dot_general(jnp.where(hsel, q, 0.0).astype(BF16), kk, _NT, preferred_element_type=F32)
                   for hsel in (head0, jnp.logical_not(head0))]
            work.append((sb, rows_q, rows_kv, scs))
        for sb, rows_q, rows_kv, scs in work:
            vv = vbuf[rows_kv, :].astype(BF16)
            mask = band & ((c + (i * nsb + sb - 1) * DIL_BLOCK) >= 0)
            o_h, l_h = [], []
            for sc in scs:
                sc = jnp.where(mask, sc, NEG)
                m = jnp.max(sc, axis=1, keepdims=True)
                e = jnp.exp(sc - m)
                den = jnp.sum(e, axis=1, keepdims=True)
                pv = jnp.dot(e.astype(BF16), vv, preferred_element_type=F32)
                o_h.append(pv / den)
                l_h.append(jnp.broadcast_to(m + jnp.log(den), (DIL_BLOCK, LANES)))
            o_ref[0, rows_q, :] = jnp.where(head0, o_h[0], o_h[1])
            l_ref[0, rows_q, :] = jnp.where(head0, l_h[0], l_h[1])
        return carry

    lax.fori_loop(0, DIL_CHUNK // DIL_BLOCK // DIL_UNROLL, trip, 0)


def _dil_attn(dq, dk, dv, gi, dil):
    B, T, _ = dq.shape
    span = DIL_BLOCK * dil
    assert T % DIL_CHUNK == 0 and DIL_CHUNK % span == 0
    cur = pl.BlockSpec((1, DIL_CHUNK, LANES), lambda b, i: (b, i, gi))
    prev = pl.BlockSpec((1, span, LANES), lambda b, i: (b, jnp.maximum(i * (DIL_CHUNK // span) - 1, 0), gi))
    out = pl.BlockSpec((1, DIL_CHUNK, LANES), lambda b, i: (b, i, 0))
    return pl.pallas_call(
        functools.partial(_dil_body, dil=dil),
        grid=(B, T // DIL_CHUNK),
        in_specs=[cur, prev, cur, prev, cur],
        out_specs=(out, out),
        out_shape=(jax.ShapeDtypeStruct((B, T, LANES), F32), jax.ShapeDtypeStruct((B, T, LANES), F32)),
        scratch_shapes=[pltpu.VMEM((span + DIL_CHUNK, LANES), F32), pltpu.VMEM((span + DIL_CHUNK, LANES), F32)],
        compiler_params=_cparams(("arbitrary", "arbitrary")),
        name="dilated_attn",
    )(dq, dk, dk, dv, dv)


def _dilated(dq, dk, dv):
    outs, lses = [], []
    for gi, (window, dil) in enumerate(DIL_PAIRS):
        assert window == DIL_BLOCK * dil
        o, l = _dil_attn(dq, dk, dv, gi, dil)
        outs.append(o)
        lses.append(l)
    return outs, lses


def _gate_expand():
    m = np.zeros((3, LANES, NSA_W), np.float32)
    for j in range(3):
        for h in range(NSA_HEADS):
            m[j, h * 3 + j, h * HEAD_DIM:(h + 1) * HEAD_DIM] = 1.0
    return jnp.asarray(m).astype(BF16)


def _out_body(h_ref, conv_ref, halo_ref, oc_ref, os_ref, ow_ref, ng_ref, nz_ref,
              o0_ref, o1_ref, o2_ref, l0_ref, l1_ref, l2_ref, dz_ref, p_ref,
              wout_ref, wgate_ref, wproj_ref, cw_ref, cb_ref, gexp_ref, nf_ref, out_ref, *, tm, final):
    i = pl.program_id(1)
    cv = conv_ref[0]
    a_b, a_c, a_h, a_z = cv[:, 0:256], cv[:, 256:512], cv[:, 512:768], cv[:, 768:1024]
    u = a_c * a_h
    hl = halo_ref[0]
    uh = hl[:, 256:512] * hl[:, 512:768] * (i > 0).astype(F32)
    rowi = lax.broadcasted_iota(jnp.int32, (tm, CONV_W), 0)
    u1 = jnp.where(rowi == 0, uh[7:8, :], pltpu.roll(u, 1, 0))
    u2 = jnp.where(rowi == 0, uh[6:7, :], jnp.where(rowi == 1, uh[7:8, :], pltpu.roll(u, 2, 0)))
    cw = cw_ref[...]
    conv = cw[0:1, :] * u2 + cw[1:2, :] * u1 + cw[2:3, :] * u + cb_ref[...]
    y_a = a_b * conv * jax.nn.silu(a_z)
    gates = jax.nn.sigmoid(ng_ref[0])
    ghi, glo = _split_bf16(gates)
    y_b = jnp.zeros((tm, NSA_W), F32)
    for j, o_ref in enumerate((oc_ref, os_ref, ow_ref)):
        gj = (jnp.dot(ghi, gexp_ref[j], preferred_element_type=F32)
              + jnp.dot(glo, gexp_ref[j], preferred_element_type=F32))
        y_b = y_b + gj * o_ref[0].astype(F32)
    y_b = y_b * jax.nn.silu(nz_ref[0])
    l0, l1, l2 = l0_ref[0], l1_ref[0], l2_ref[0]
    mx = jnp.maximum(jnp.maximum(l0, l1), l2)
    e0, e1, e2 = jnp.exp(l0 - mx), jnp.exp(l1 - mx), jnp.exp(l2 - mx)
    den = e0 + e1 + e2
    y_c = jnp.concatenate([(e0 / den) * o0_ref[0], (e1 / den) * o1_ref[0], (e2 / den) * o2_ref[0]], axis=1)
    y_c = y_c * jax.nn.silu(dz_ref[0])
    y = jnp.concatenate([y_a, y_b, y_c], axis=1).astype(BF16)
    h1 = h_ref[0] + jnp.dot(y, wout_ref[...], preferred_element_type=F32)
    gate = jax.nn.sigmoid(jnp.dot(h1.astype(BF16), wgate_ref[...], preferred_element_type=F32))
    pp = jnp.dot(p_ref[0, 0].astype(BF16), wproj_ref[...], preferred_element_type=F32)
    h2 = h1 + gate * pp
    if final:
        ms = jnp.mean(h2 * h2, axis=-1, keepdims=True)
        h2 = h2 * lax.rsqrt(ms + RMS_EPS) * nf_ref[...]
    out_ref[0] = h2


def _out_proj(h, conv, oc, os_, ow, ng, nz, outs, lses, dz, p, layer, wout, wgate, wproj, cw, cb, nf, final):
    B, T, D = h.shape
    tm = min(ROW_TILE, T)
    row = lambda n: pl.BlockSpec((1, tm, n), lambda b, i: (b, i, 0))
    full = lambda shape: pl.BlockSpec(shape, lambda b, i: (0,) * len(shape))
    halo = pl.BlockSpec((1, 8, 1024), lambda b, i: (b, jnp.maximum(i * (tm // 8) - 1, 0), 0))
    ple = pl.BlockSpec((1, 1, tm, PLE_DIM), lambda b, i: (layer, b, i, 0))
    in_specs = [row(D), row(1024), halo, row(384), row(384), row(384), row(128), row(384),
                row(128), row(128), row(128), row(128), row(128), row(128), row(384), ple,
                full((D, D)), full((D, D)), full((PLE_DIM, D)), full((8, CONV_W)), full((1, CONV_W)),
                full((3, LANES, NSA_W)), full((1, D))]
    return pl.pallas_call(
        functools.partial(_out_body, tm=tm, final=final),
        grid=(B, T // tm),
        in_specs=in_specs,
        out_specs=row(D),
        out_shape=jax.ShapeDtypeStruct((B, T, D), F32),
        compiler_params=_cparams(("arbitrary", "arbitrary")),
        name="out_proj",
    )(h, conv, conv, oc, os_, ow, ng, nz, *outs, *lses, dz, p, wout, wgate, wproj, cw, cb,
      _gate_expand(), nf)


def kernel(x, p, norm_mix, w_in, conv_w, conv_b, cmp_pe, cmp_w1, cmp_w2, w_out, w_ple_gate, w_ple_proj,
           norm_final):
    B, T, D = x.shape
    depth = w_in.shape[0]
    assert D == D_MODEL and T % 2048 == 0 and T // SLC_BLOCK <= N_SLC_PAD
    h = x
    for i in range(depth):
        w = _permute_w_in(w_in[i])
        (conv, qa, kaug, vspT, kwp, vwT, kc, vc, ng, nz, dq, dk, dv, dz) = _inproj(
            h, norm_mix[i].reshape(1, D), w)
        kcmp, vcmpT = _compress(kc, vc, *_compress_weights(cmp_pe[i], cmp_w1[i], cmp_w2[i]))
        oc, selb = _cmp_select(qa, kcmp, vcmpT)
        os_ = _slc_attn(qa, selb, kaug, vspT)
        ow = _win_attn(qa, kwp, vwT)
        outs, lses = _dilated(dq, dk, dv)
        cw = jnp.zeros((8, CONV_W), F32).at[0:CONV_K].set(conv_w[i])
        h = _out_proj(h, conv, oc, os_, ow, ng, nz, outs, lses, dz, p, i,
                      w_out[i].astype(BF16), w_ple_gate[i].astype(BF16), w_ple_proj[i].astype(BF16),
                      cw, conv_b[i].reshape(1, CONV_W), norm_final.reshape(1, D), final=(i == depth - 1))
    return h
```

```python
import functools

import numpy as np
import jax
import jax.numpy as jnp
from jax import lax
from jax.experimental import pallas as pl
from jax.experimental.pallas import tpu as pltpu

F32 = jnp.float32
BF16 = jnp.bfloat16

D_MODEL = 1024
PLE_DIM = 256
HEAD_DIM = 64
RMS_EPS = 1e-6
CONV_W = 256
CONV_K = 3
NSA_HEADS = 6
NSA_KV_HEADS = 2
NSA_GROUP = 3
NSA_W = 384
CMP_BLOCK = 32
CMP_STRIDE = 16
SLC_BLOCK = 64
SLC_TOPK = 16
WIN = 512
FORCE_SCORE = 1e4
DIL_PAIRS = ((128, 1), (512, 4), (2048, 16))
DIL_W = 384
DIL_BLOCK = 128
IN_SIZES = (256,) * 4 + (384,) + (128,) * 6 + (18, 384) + (384,) * 4

LANES = 128
MXU_COLS = 256
N_SLC_PAD = 128
Q_TILE = 128
SLC_Q_TILE = 256
CMP_Q_TILE = 512
CMP_CLASSES = 4
WIN_Q_TILE = 256
V_ROWS = 80
KV_TILE = 256
ROW_TILE = 512
NEG = -1e30
SCALE = HEAD_DIM ** -0.5
LOG2E = float(np.log2(np.e))
VMEM_LIMIT = 56 * 1024 * 1024

_NT = (((1,), (1,)), ((), ()))


def _cparams(sem):
    return pltpu.CompilerParams(dimension_semantics=sem, vmem_limit_bytes=VMEM_LIMIT)


_C_CONV = 0
_C_NQ = 1024
_C_KV = _C_NQ + 384
_C_KC = _C_KV + 512
_C_VC = _C_KC + 128
_C_NG = _C_VC + 128
_C_NZ = _C_NG + 128
_C_DQ = _C_NZ + 384
_C_END = _C_DQ + 4 * 384


def _permute_w_in(w):
    offs = np.concatenate([[0], np.cumsum(IN_SIZES)]).astype(int)
    o = lambda k: int(offs[k])
    nq = w[:, o(4):o(5)] * (SCALE * LOG2E)
    dq = w[:, o(13):o(14)] * SCALE
    pad = jnp.zeros((w.shape[0], LANES - 18), w.dtype)
    out = jnp.concatenate([w[:, o(0):o(4)], nq, w[:, o(7):o(11)], w[:, o(5):o(7)], w[:, o(11):o(12)], pad,
                           w[:, o(12):o(13)], dq, w[:, o(14):o(17)]], axis=1)
    assert out.shape[1] == _C_END
    return out.astype(BF16)


def _head_lanes(c, lane_lo, odd, fill):
    if odd:
        c = pltpu.roll(c, HEAD_DIM, 1)
    return jnp.where(lane_lo, c, fill)


def _inproj_body(x_ref, g_ref, w_ref, conv_ref, qa_ref, kaug_ref, vspT_ref, kwp_ref, vwT_ref,
                 kc_ref, vc_ref, ng_ref, nz_ref, dq_ref, dk_ref, dv_ref, dz_ref, *, tm):
    x = x_ref[0]
    ms = jnp.mean(x * x, axis=-1, keepdims=True)
    xn = (x * lax.rsqrt(ms + RMS_EPS) * g_ref[...]).astype(BF16)

    def seg(a, b):
        return jnp.dot(xn, w_ref[:, a:b], preferred_element_type=F32)

    conv_ref[0] = seg(_C_CONV, _C_NQ)

    t0 = pl.program_id(1) * tm
    row = lax.broadcasted_iota(jnp.int32, (tm, LANES), 0) + t0
    lane = lax.broadcasted_iota(jnp.int32, (tm, LANES), 1)
    onehot = jnp.where((row >> 6) == lane, 1.0, 0.0).astype(BF16)
    lane_lo = lane < HEAD_DIM

    q = seg(_C_NQ, _C_KV)
    for h in range(NSA_HEADS):
        c = q[:, (h // 2) * LANES:(h // 2 + 1) * LANES]
        qa_ref[0, h] = _head_lanes(c, lane_lo, h % 2 == 1, 0.0).astype(BF16)

    kv = seg(_C_KV, _C_KC)
    for g in range(NSA_KV_HEADS):
        ks = _head_lanes(kv[:, 0:128], lane_lo, g == 1, 0.0).astype(BF16)
        kaug_ref[0, g] = jnp.concatenate([onehot, ks], axis=1)
        vspT_ref[0, g] = _head_lanes(kv[:, 128:256], lane_lo, g == 1, 1.0).T[0:V_ROWS, :].astype(BF16)
        kwp_ref[0, g] = _head_lanes(kv[:, 256:384], lane_lo, g == 1, 0.0).astype(BF16)
        vwT_ref[0, g] = _head_lanes(kv[:, 384:512], lane_lo, g == 1, 1.0).T[0:V_ROWS, :].astype(BF16)

    misc = seg(_C_KC, _C_DQ)
    kc_ref[0] = misc[:, 0:128]
    vc_ref[0] = misc[:, 128:256]
    ng_ref[0] = misc[:, 256:384]
    nz_ref[0] = misc[:, 384:768]
    dil = seg(_C_DQ, _C_END)
    dq_ref[0] = dil[:, 0:384]
    dk_ref[0] = dil[:, 384:768]
    dv_ref[0] = dil[:, 768:1152]
    dz_ref[0] = dil[:, 1152:1536]


def _inproj(h, g, w):
    B, T, D = h.shape
    tm = min(ROW_TILE, T)
    assert T % tm == 0
    row = lambda n: pl.BlockSpec((1, tm, n), lambda b, i: (b, i, 0))
    head = lambda nh, n: pl.BlockSpec((1, nh, tm, n), lambda b, i: (b, 0, i, 0))
    sd = jax.ShapeDtypeStruct
    out_shape = (
        sd((B, T, 1024), F32), sd((B, 6, T, LANES), BF16), sd((B, 2, T, 2 * LANES), BF16),
        sd((B, 2, V_ROWS, T), BF16), sd((B, 2, T, LANES), BF16), sd((B, 2, V_ROWS, T), BF16),
        sd((B, T, 128), F32), sd((B, T, 128), F32), sd((B, T, 128), F32), sd((B, T, 384), F32),
        sd((B, T, 384), F32), sd((B, T, 384), F32), sd((B, T, 384), F32), sd((B, T, 384), F32))
    headT = pl.BlockSpec((1, 2, V_ROWS, tm), lambda b, i: (b, 0, 0, i))
    out_specs = (row(1024), head(6, LANES), head(2, 2 * LANES), headT, head(2, LANES),
                 headT, row(128), row(128), row(128), row(384),
                 row(384), row(384), row(384), row(384))
    return pl.pallas_call(
        functools.partial(_inproj_body, tm=tm),
        grid=(B, T // tm),
        in_specs=[row(D), pl.BlockSpec((1, D), lambda b, i: (0, 0)),
                  pl.BlockSpec((D, _C_END), lambda b, i: (0, 0))],
        out_specs=out_specs, out_shape=out_shape,
        compiler_params=_cparams(("arbitrary", "arbitrary")),
        name="inproj",
    )(h, g, w)


def _compress_weights(pe, w1, w2):
    eye = jnp.eye(NSA_KV_HEADS, dtype=F32)
    pe4 = pe.reshape(2, 2, 16, 1, HEAD_DIM)
    pe_t = jnp.broadcast_to(pe4, (2, 2, 16, NSA_KV_HEADS, HEAD_DIM)).reshape(2, 2, 16, 1, 128)
    w1r = w1.reshape(2, 2, 16, HEAD_DIM, HEAD_DIM)
    w1b = jnp.einsum("khmeo,gf->khmgefo", w1r, eye).reshape(2, 2, 16, 128, 128)
    w2k = jnp.zeros((NSA_KV_HEADS, 128, 128), F32)
    for g in range(NSA_KV_HEADS):
        w2k = w2k.at[g, g * 64:(g + 1) * 64, 0:64].set(w2[0])
    w2v = jnp.einsum("eo,gf->gefo", w2[1], eye).reshape(128, 128)
    return pe_t.astype(F32), w1b.astype(BF16), w2k.astype(BF16), w2v.astype(BF16)


def _compress_body(kc_ref, vc_ref, pe_ref, w1_ref, w2k_ref, w2v_ref, kcmp_ref, vcmpT_ref, *, nc):
    def hidden(x_ref, kv):
        a0 = jnp.zeros((nc, LANES), F32)
        a1 = jnp.zeros((nc, LANES), F32)
        for m in range(CMP_STRIDE):
            z = x_ref[0, pl.ds(m, nc, stride=CMP_STRIDE), :]
            a0 = a0 + jnp.dot((z + pe_ref[kv, 0, m]).astype(BF16), w1_ref[kv, 0, m],
                              preferred_element_type=F32)
            a1 = a1 + jnp.dot((z + pe_ref[kv, 1, m]).astype(BF16), w1_ref[kv, 1, m],
                              preferred_element_type=F32)
        hid = a0 + pltpu.roll(a1, nc - 1, 0)
        return jax.nn.silu(hid).astype(BF16)

    hk = hidden(kc_ref, 0)
    for g in range(NSA_KV_HEADS):
        kcmp_ref[0, g] = jnp.dot(hk, w2k_ref[g], preferred_element_type=F32).astype(BF16)
    hv = hidden(vc_ref, 1)
    vout = jnp.dot(hv, w2v_ref[...], preferred_element_type=F32)
    vcmpT_ref[0] = vout.T.astype(BF16)


def _compress(kc, vc, pe_t, w1b, w2k, w2v):
    B, T, _ = kc.shape
    nc = T // CMP_STRIDE
    full = lambda shape: pl.BlockSpec(shape, lambda b: (0,) * len(shape))
    return pl.pallas_call(
        functools.partial(_compress_body, nc=nc),
        grid=(B,),
        in_specs=[pl.BlockSpec((1, T, LANES), lambda b: (b, 0, 0)),
                  pl.BlockSpec((1, T, LANES), lambda b: (b, 0, 0)),
                  full((2, 2, 16, 1, 128)), full((2, 2, 16, 128, 128)), full((2, 128, 128)), full((128, 128))],
        out_specs=(pl.BlockSpec((1, 2, nc, 128), lambda b: (b, 0, 0, 0)),
                   pl.BlockSpec((1, 128, nc), lambda b: (b, 0, 0))),
        out_shape=(jax.ShapeDtypeStruct((B, 2, nc, 128), BF16),
                   jax.ShapeDtypeStruct((B, 128, nc), BF16)),
        compiler_params=_cparams(("arbitrary",)),
        name="nsa_compress",
    )(kc, vc, pe_t, w1b, w2k, w2v)


def _split_bf16(x):
    hi = x.astype(BF16)
    lo = (x - hi.astype(F32)).astype(BF16)
    return hi, lo


def _cmp_select_body(qa_ref, kcmp_ref, vcmpT_ref, ovT_ref, oc_ref, selb_ref, *, nc, qt, n_tiles_static):
    i = pl.program_id(1)
    q0 = i * qt

    def run(nrow, nblk):
        jj = lax.broadcasted_iota(jnp.int32, (nrow, qt), 0)
        tt = lax.broadcasted_iota(jnp.int32, (nrow, qt), 1) + q0
        cmask = (jj * CMP_STRIDE + (CMP_BLOCK - 1)) <= tt
        blk = lax.broadcasted_iota(jnp.int32, (nblk, qt), 0)
        ts = lax.broadcasted_iota(jnp.int32, (nblk, qt), 1) + q0
        cur = ts >> 6
        forced = (blk == 0) | (blk == cur) | (blk == cur - 1)
        causal = blk * SLC_BLOCK <= ts
        blkf = blk.astype(F32)
        ovT = ovT_ref[0:nblk, 0:nrow]

        raw = [lax.dot_general(kcmp_ref[0, h // NSA_GROUP, 0:nrow, :], qa_ref[0, h], _NT,
                               preferred_element_type=F32) for h in range(NSA_HEADS)]
        ocT = []
        for g in range(NSA_KV_HEADS):
            psum = jnp.zeros((nrow, qt), F32)
            for r in range(NSA_GROUP):
                h = g * NSA_GROUP + r
                s = jnp.where(cmask, raw[h], -jnp.inf)
                m = jnp.max(s, axis=0, keepdims=True)
                m = jnp.where(m > -jnp.inf, m, 0.0)
                e = jnp.exp2(s - m)
                den = jnp.sum(e, axis=0, keepdims=True)
                rden = 1.0 / jnp.maximum(den, 1e-30)
                ocT.append(jnp.dot(vcmpT_ref[0, g * 64:(g + 1) * 64, 0:nrow], e.astype(BF16),
                                   preferred_element_type=F32) * rden)
                psum = psum + e * rden
            hi, lo = _split_bf16(psum)
            imp = (jnp.dot(ovT, hi, preferred_element_type=F32)
                   + jnp.dot(ovT, lo, preferred_element_type=F32))
            imp = jnp.where(causal, imp, -1.0)
            imp = jnp.where(forced, -2.0, imp)
            sel = jnp.where(forced, 1.0, 0.0)
            for _ in range(SLC_TOPK - 3):
                mx = jnp.max(imp, axis=0, keepdims=True)
                first = jnp.min(jnp.where(imp == mx, blkf, float(N_SLC_PAD)), axis=0, keepdims=True)
                hit = blkf == first
                sel = jnp.where(hit, 1.0, sel)
                imp = jnp.where(hit, -2.0, imp)
            bias = jnp.where(sel > 0.5, 0.0, NEG)
            if nblk < N_SLC_PAD:
                bias = jnp.concatenate([bias, jnp.full((N_SLC_PAD - nblk, qt), NEG, F32)], axis=0)
            for sub in range(qt // Q_TILE):
                selb_ref[0, g, sub * Q_TILE:(sub + 1) * Q_TILE, :] = (
                    bias[:, sub * Q_TILE:(sub + 1) * Q_TILE].T.astype(BF16))
        for pr in range(NSA_HEADS // 2):
            pair = jnp.concatenate([ocT[2 * pr], ocT[2 * pr + 1]], axis=0)
            for sub in range(qt // Q_TILE):
                oc_ref[0, sub * Q_TILE:(sub + 1) * Q_TILE, pr * LANES:(pr + 1) * LANES] = (
                    pair[:, sub * Q_TILE:(sub + 1) * Q_TILE].T.astype(oc_ref.dtype))

    bounds = [(k * n_tiles_static) // CMP_CLASSES for k in range(CMP_CLASSES + 1)]
    for k in range(CMP_CLASSES):
        lo, hi = bounds[k], bounds[k + 1]
        if hi == lo:
            continue
        nrow = min(nc, -(-(hi * qt // CMP_STRIDE) // LANES) * LANES)
        nblk = min(N_SLC_PAD, max(32, -(-(hi * qt // SLC_BLOCK) // 8) * 8))
        pl.when((i >= lo) & (i < hi))(functools.partial(run, nrow, nblk))


def _overlap_T(nc):
    j = np.arange(nc)[None, :] * CMP_STRIDE
    s = np.arange(N_SLC_PAD)[:, None] * SLC_BLOCK
    ov = (j < s + SLC_BLOCK) & (j + CMP_BLOCK > s) & (np.arange(nc)[None, :] < nc - 1)
    return jnp.asarray(ov.astype(np.float32)).astype(BF16)


def _cmp_select(qa, kcmp, vcmpT):
    B, _, T, _ = qa.shape
    nc = T // CMP_STRIDE
    qt = CMP_Q_TILE
    return pl.pallas_call(
        functools.partial(_cmp_select_body, nc=nc, qt=qt, n_tiles_static=T // qt),
        grid=(B, T // qt),
        in_specs=[pl.BlockSpec((1, 6, qt, LANES), lambda b, i: (b, 0, i, 0)),
                  pl.BlockSpec((1, 2, nc, LANES), lambda b, i: (b, 0, 0, 0)),
                  pl.BlockSpec((1, LANES, nc), lambda b, i: (b, 0, 0)),
                  pl.BlockSpec((N_SLC_PAD, nc), lambda b, i: (0, 0))],
        out_specs=(pl.BlockSpec((1, qt, NSA_W), lambda b, i: (b, i, 0)),
                   pl.BlockSpec((1, 2, qt, N_SLC_PAD), lambda b, i: (b, 0, i, 0))),
        out_shape=(jax.ShapeDtypeStruct((B, T, NSA_W), BF16),
                   jax.ShapeDtypeStruct((B, 2, T, N_SLC_PAD), BF16)),
        compiler_params=_cparams(("arbitrary", "arbitrary")),
        name="nsa_cmp_select",
    )(qa, kcmp, vcmpT, _overlap_T(nc))


M_INIT = -1e29


def _store_heads_T(acc_ref, o_ref, qs):
    for sub in range(qs // Q_TILE):
        outs = []
        for g in range(NSA_KV_HEADS):
            for r in range(NSA_GROUP):
                lo = r * qs + sub * Q_TILE
                a = acc_ref[g, :, lo:lo + Q_TILE]
                outs.append(a[0:HEAD_DIM, :] / a[HEAD_DIM:HEAD_DIM + 1, :])
        for pr in range(NSA_HEADS // 2):
            pair = jnp.concatenate([outs[2 * pr], outs[2 * pr + 1]], axis=0)
            o_ref[0, sub * Q_TILE:(sub + 1) * Q_TILE, pr * LANES:(pr + 1) * LANES] = pair.T.astype(o_ref.dtype)


def _slc_body(qa_ref, selb_ref, kaug_ref, vspT_ref, o_ref, acc_ref, s_ref, p_ref, al_ref, qT_ref, *, qs):
    i = pl.program_id(1)
    q0 = i * qs
    rows = NSA_GROUP * qs
    nfull = q0 // KV_TILE
    t_len = kaug_ref.shape[2]
    for g in range(NSA_KV_HEADS):
        bias = selb_ref[0, g]
        for r in range(NSA_GROUP):
            qb = jnp.concatenate([bias, qa_ref[0, g * NSA_GROUP + r]], axis=1)
            for c in range(qs // MXU_COLS):
                lo = c * MXU_COLS
                qT_ref[g, r, :, lo:lo + MXU_COLS] = qb[lo:lo + MXU_COLS, :].T
    acc_ref[...] = jnp.zeros(acc_ref.shape, F32)
    s_ref[...] = jnp.full(s_ref.shape, NEG, F32)
    p_ref[...] = jnp.zeros(p_ref.shape, BF16)
    al_ref[...] = jnp.ones(al_ref.shape, F32)
    kpos = lax.broadcasted_iota(jnp.int32, (KV_TILE, MXU_COLS), 0)
    tq = lax.broadcasted_iota(jnp.int32, (KV_TILE, MXU_COLS), 1) + q0

    def trip(j, par, ms, qk, softmax):
        st_c = pl.multiple_of(jnp.maximum(j - 2, 0) * KV_TILE, KV_TILE)
        st_a = pl.multiple_of(jnp.minimum(j * KV_TILE, t_len - KV_TILE), KV_TILE)
        out = []
        for g in range(NSA_KV_HEADS):
            if qk is not None:
                kt = kaug_ref[0, g, pl.ds(st_a, KV_TILE), :]
                for r in range(NSA_GROUP):
                    for c in range(qs // MXU_COLS):
                        lo = c * MXU_COLS
                        s = jnp.dot(kt, qT_ref[g, r, :, lo:lo + MXU_COLS], preferred_element_type=F32)
                        if qk == "masked":
                            s = jnp.where(kpos + j * KV_TILE <= tq + lo, s, NEG)
                        s_ref[par, g, :, r * qs + lo:r * qs + lo + MXU_COLS] = s
        for g in range(NSA_KV_HEADS):
            vt = vspT_ref[0, g, :, pl.ds(st_c, KV_TILE)]
            for c in range(rows // MXU_COLS):
                cols = slice(c * MXU_COLS, (c + 1) * MXU_COLS)
                acc_ref[g, :, cols] = al_ref[par, g, :, cols] * acc_ref[g, :, cols] + jnp.dot(
                    vt, p_ref[par, g, :, cols], preferred_element_type=F32)
        for g in range(NSA_KV_HEADS):
            m_new = ms[g]
            if softmax:
                s = s_ref[1 - par, g]
                m_new = jnp.maximum(ms[g], jnp.max(s, axis=0, keepdims=True))
                al_ref[1 - par, g] = jnp.exp2(ms[g] - m_new)
                p_ref[1 - par, g] = jnp.exp2(s - m_new).astype(BF16)
            out.append(m_new)
        return tuple(out)

    def pair(k, ms):
        ms = trip(2 * k, 0, ms, "plain", True)
        return trip(2 * k + 1, 1, ms, "plain", True)

    def quad(k, ms):
        return pair(2 * k + 1, pair(2 * k, ms))

    npairs = nfull // 2
    nquads = npairs // 2
    ms = tuple(jnp.full((1, rows), M_INIT, F32) for _ in range(NSA_KV_HEADS))
    ms = lax.fori_loop(0, nquads, quad, ms)
    ms = lax.fori_loop(2 * nquads, npairs, pair, ms)
    t0 = 2 * npairs
    nmask = max(2, qs // KV_TILE)
    for d in range(nmask):
        ms = trip(t0 + d, d % 2, ms, "masked", True)
    ms = trip(t0 + nmask, nmask % 2, ms, None, True)
    trip(t0 + nmask + 1, (nmask + 1) % 2, ms, None, False)
    _store_heads_T(acc_ref, o_ref, qs)


def _slc_attn(qa, selb, kaug, vspT, qs=SLC_Q_TILE):
    B, _, T, _ = qa.shape
    assert T % qs == 0 and (KV_TILE % qs == 0 or qs % (2 * KV_TILE) == 0)
    rows = NSA_GROUP * qs
    return pl.pallas_call(
        functools.partial(_slc_body, qs=qs),
        grid=(B, T // qs),
        in_specs=[pl.BlockSpec((1, 6, qs, LANES), lambda b, i: (b, 0, i, 0)),
                  pl.BlockSpec((1, 2, qs, N_SLC_PAD), lambda b, i: (b, 0, i, 0)),
                  pl.BlockSpec((1, 2, T, 2 * LANES), lambda b, i: (b, 0, 0, 0)),
                  pl.BlockSpec((1, 2, V_ROWS, T), lambda b, i: (b, 0, 0, 0))],
        out_specs=pl.BlockSpec((1, qs, NSA_W), lambda b, i: (b, i, 0)),
        out_shape=jax.ShapeDtypeStruct((B, T, NSA_W), BF16),
        scratch_shapes=[pltpu.VMEM((NSA_KV_HEADS, V_ROWS, rows), F32),
                        pltpu.VMEM((2, NSA_KV_HEADS, KV_TILE, rows), F32),
                        pltpu.VMEM((2, NSA_KV_HEADS, KV_TILE, rows), BF16),
                        pltpu.VMEM((2, NSA_KV_HEADS, 1, rows), F32),
                        pltpu.VMEM((NSA_KV_HEADS, NSA_GROUP, 2 * LANES, qs), BF16)],
        compiler_params=_cparams(("arbitrary", "arbitrary")),
        name="nsa_slc_attn",
    )(qa, selb, kaug, vspT)


def _win_body(qa_ref, kwp_ref, vwT_ref, o_ref, acc_ref, *, qs):
    i = pl.program_id(1)
    q0 = i * qs
    rows = NSA_GROUP * qs
    ntile = WIN // qs + 1
    kk = lax.broadcasted_iota(jnp.int32, (qs, rows), 0)
    tq = (lax.broadcasted_iota(jnp.int32, (qs, rows), 1) & (qs - 1)) + q0
    starts = [pl.multiple_of(jnp.maximum(i - (ntile - 1) + d, 0) * qs, qs) for d in range(ntile)]
    raw = []
    for g in range(NSA_KV_HEADS):
        qrows = jnp.concatenate([qa_ref[0, g * NSA_GROUP + r] for r in range(NSA_GROUP)], axis=0)
        raw.append([lax.dot_general(kwp_ref[0, g, pl.ds(starts[d], qs), :], qrows, _NT,
                                    preferred_element_type=F32) for d in range(ntile)])
    for g in range(NSA_KV_HEADS):
        scores = []
        for d in range(ntile):
            kpos = kk + (i - (ntile - 1) + d) * qs
            if d == 0:
                mask = (kpos >= 0) & ((tq - kpos) < WIN)
            elif d == ntile - 1:
                mask = kpos <= tq
            else:
                mask = kpos >= 0
            scores.append(jnp.where(mask, raw[g][d], NEG))
        m = functools.reduce(jnp.maximum, [jnp.max(s, axis=0, keepdims=True) for s in scores])
        acc = jnp.zeros((V_ROWS, rows), F32)
        for s, st in zip(scores, starts):
            acc = acc + jnp.dot(vwT_ref[0, g, :, pl.ds(st, qs)], jnp.exp2(s - m).astype(BF16),
                                preferred_element_type=F32)
        acc_ref[g] = acc
    _store_heads_T(acc_ref, o_ref, qs)


def _win_attn(qa, kwp, vwT, qs=WIN_Q_TILE):
    B, _, T, _ = qa.shape
    assert WIN % qs == 0 and T % qs == 0
    rows = NSA_GROUP * qs
    return pl.pallas_call(
        functools.partial(_win_body, qs=qs),
        grid=(B, T // qs),
        in_specs=[pl.BlockSpec((1, 6, qs, LANES), lambda b, i: (b, 0, i, 0)),
                  pl.BlockSpec((1, 2, T, LANES), lambda b, i: (b, 0, 0, 0)),
                  pl.BlockSpec((1, 2, V_ROWS, T), lambda b, i: (b, 0, 0, 0))],
        out_specs=pl.BlockSpec((1, qs, NSA_W), lambda b, i: (b, i, 0)),
        out_shape=jax.ShapeDtypeStruct((B, T, NSA_W), BF16),
        scratch_shapes=[pltpu.VMEM((NSA_KV_HEADS, V_ROWS, rows), F32)],
        compiler_params=_cparams(("arbitrary", "arbitrary")),
        name="nsa_win_attn",
    )(qa, kwp, vwT)


DIL_CHUNK = 2048
DIL_UNROLL = 4


def _dil_body(q_ref, kp_ref, kc_ref, vp_ref, vc_ref, o_ref, l_ref, kbuf, vbuf, *, dil):
    i = pl.program_id(1)
    span = DIL_BLOCK * dil
    nsb = DIL_CHUNK // span
    kbuf[0:span, :] = kp_ref[0]
    kbuf[span:span + DIL_CHUNK, :] = kc_ref[0]
    vbuf[0:span, :] = vp_ref[0]
    vbuf[span:span + DIL_CHUNK, :] = vc_ref[0]
    a = lax.broadcasted_iota(jnp.int32, (DIL_BLOCK, 2 * DIL_BLOCK), 0)
    c = lax.broadcasted_iota(jnp.int32, (DIL_BLOCK, 2 * DIL_BLOCK), 1)
    dist = a + DIL_BLOCK - c
    band = (dist >= 0) & (dist <= DIL_BLOCK)
    head0 = lax.broadcasted_iota(jnp.int32, (DIL_BLOCK, LANES), 1) < HEAD_DIM
    shift = dil.bit_length() - 1

    def trip(t, carry):
        work = []
        for k in range(DIL_UNROLL):
            u = t * DIL_UNROLL + k
            sb = u >> shift
            r = u & (dil - 1)
            row0 = sb * span + r
            if dil == 1:
                rows_q = pl.ds(row0, DIL_BLOCK)
                rows_kv = pl.ds(row0, 2 * DIL_BLOCK)
            else:
                rows_q = pl.ds(row0, DIL_BLOCK, stride=dil)
                rows_kv = pl.ds(row0, 2 * DIL_BLOCK, stride=dil)
            q = q_ref[0, rows_q, :]
            kk = kbuf[rows_kv, :].astype(BF16)
            scs = [lax.dot_general(jnp.where(hsel, q, 0.0).astype(BF16), kk, _NT, preferred_element_type=F32)
                   for hsel in (head0, jnp.logical_not(head0))]
            work.append((sb, rows_q, rows_kv, scs))
        for sb, rows_q, rows_kv, scs in work:
            vv = vbuf[rows_kv, :].astype(BF16)
            mask = band & ((c + (i * nsb + sb - 1) * DIL_BLOCK) >= 0)
            o_h, l_h = [], []
            for sc in scs:
                sc = jnp.where(mask, sc, NEG)
                m = jnp.max(sc, axis=1, keepdims=True)
                e = jnp.exp(sc - m)
                den = jnp.sum(e, axis=1, keepdims=True)
                pv = jnp.dot(e.astype(BF16), vv, preferred_element_type=F32)
                o_h.append(pv / den)
                l_h.append(jnp.broadcast_to(m + jnp.log(den), (DIL_BLOCK, LANES)))
            o_ref[0, rows_q, :] = jnp.where(head0, o_h[0], o_h[1])
            l_ref[0, rows_q, :] = jnp.where(head0, l_h[0], l_h[1])
        return carry

    lax.fori_loop(0, DIL_CHUNK // DIL_BLOCK // DIL_UNROLL, trip, 0)


def _dil_attn(dq, dk, dv, gi, dil):
    B, T, _ = dq.shape
    span = DIL_BLOCK * dil
    assert T % DIL_CHUNK == 0 and DIL_CHUNK % span == 0
    cur = pl.BlockSpec((1, DIL_CHUNK, LANES), lambda b, i: (b, i, gi))
    prev = pl.BlockSpec((1, span, LANES), lambda b, i: (b, jnp.maximum(i * (DIL_CHUNK // span) - 1, 0), gi))
    out = pl.BlockSpec((1, DIL_CHUNK, LANES), lambda b, i: (b, i, 0))
    return pl.pallas_call(
        functools.partial(_dil_body, dil=dil),
        grid=(B, T // DIL_CHUNK),
        in_specs=[cur, prev, cur, prev, cur],
        out_specs=(out, out),
        out_shape=(jax.ShapeDtypeStruct((B, T, LANES), F32), jax.ShapeDtypeStruct((B, T, LANES), F32)),
        scratch_shapes=[pltpu.VMEM((span + DIL_CHUNK, LANES), F32), pltpu.VMEM((span + DIL_CHUNK, LANES), F32)],
        compiler_params=_cparams(("arbitrary", "arbitrary")),
        name="dilated_attn",
    )(dq, dk, dk, dv, dv)


def _dilated(dq, dk, dv):
    outs, lses = [], []
    for gi, (window, dil) in enumerate(DIL_PAIRS):
        assert window == DIL_BLOCK * dil
        o, l = _dil_attn(dq, dk, dv, gi, dil)
        outs.append(o)
        lses.append(l)
    return outs, lses


def _gate_expand():
    m = np.zeros((3, LANES, NSA_W), np.float32)
    for j in range(3):
        for h in range(NSA_HEADS):
            m[j, h * 3 + j, h * HEAD_DIM:(h + 1) * HEAD_DIM] = 1.0
    return jnp.asarray(m).astype(BF16)


def _out_body(h_ref, conv_ref, halo_ref, oc_ref, os_ref, ow_ref, ng_ref, nz_ref,
              o0_ref, o1_ref, o2_ref, l0_ref, l1_ref, l2_ref, dz_ref, p_ref,
              wout_ref, wgate_ref, wproj_ref, cw_ref, cb_ref, gexp_ref, nf_ref, out_ref, *, tm, final):
    i = pl.program_id(1)
    cv = conv_ref[0]
    a_b, a_c, a_h, a_z = cv[:, 0:256], cv[:, 256:512], cv[:, 512:768], cv[:, 768:1024]
    u = a_c * a_h
    hl = halo_ref[0]
    uh = hl[:, 256:512] * hl[:, 512:768] * (i > 0).astype(F32)
    rowi = lax.broadcasted_iota(jnp.int32, (tm, CONV_W), 0)
    u1 = jnp.where(rowi == 0, uh[7:8, :], pltpu.roll(u, 1, 0))
    u2 = jnp.where(rowi == 0, uh[6:7, :], jnp.where(rowi == 1, uh[7:8, :], pltpu.roll(u, 2, 0)))
    cw = cw_ref[...]
    conv = cw[0:1, :] * u2 + cw[1:2, :] * u1 + cw[2:3, :] * u + cb_ref[...]
    y_a = a_b * conv * jax.nn.silu(a_z)
    gates = jax.nn.sigmoid(ng_ref[0])
    ghi, glo = _split_bf16(gates)
    y_b = jnp.zeros((tm, NSA_W), F32)
    for j, o_ref in enumerate((oc_ref, os_ref, ow_ref)):
        gj = (jnp.dot(ghi, gexp_ref[j], preferred_element_type=F32)
              + jnp.dot(glo, gexp_ref[j], preferred_element_type=F32))
        y_b = y_b + gj * o_ref[0].astype(F32)
    y_b = y_b * jax.nn.silu(nz_ref[0])
    l0, l1, l2 = l0_ref[0], l1_ref[0], l2_ref[0]
    mx = jnp.maximum(jnp.maximum(l0, l1), l2)
    e0, e1, e2 = jnp.exp(l0 - mx), jnp.exp(l1 - mx), jnp.exp(l2 - mx)
    den = e0 + e1 + e2
    y_c = jnp.concatenate([(e0 / den) * o0_ref[0], (e1 / den) * o1_ref[0], (e2 / den) * o2_ref[0]], axis=1)
    y_c = y_c * jax.nn.silu(dz_ref[0])
    y = jnp.concatenate([y_a, y_b, y_c], axis=1).astype(BF16)
    h1 = h_ref[0] + jnp.dot(y, wout_ref[...], preferred_element_type=F32)
    gate = jax.nn.sigmoid(jnp.dot(h1.astype(BF16), wgate_ref[...], preferred_element_type=F32))
    pp = jnp.dot(p_ref[0, 0].astype(BF16), wproj_ref[...], preferred_element_type=F32)
    h2 = h1 + gate * pp
    if final:
        ms = jnp.mean(h2 * h2, axis=-1, keepdims=True)
        h2 = h2 * lax.rsqrt(ms + RMS_EPS) * nf_ref[...]
    out_ref[0] = h2


def _out_proj(h, conv, oc, os_, ow, ng, nz, outs, lses, dz, p, layer, wout, wgate, wproj, cw, cb, nf, final):
    B, T, D = h.shape
    tm = min(ROW_TILE, T)
    row = lambda n: pl.BlockSpec((1, tm, n), lambda b, i: (b, i, 0))
    full = lambda shape: pl.BlockSpec(shape, lambda b, i: (0,) * len(shape))
    halo = pl.BlockSpec((1, 8, 1024), lambda b, i: (b, jnp.maximum(i * (tm // 8) - 1, 0), 0))
    ple = pl.BlockSpec((1, 1, tm, PLE_DIM), lambda b, i: (layer, b, i, 0))
    in_specs = [row(D), row(1024), halo, row(384), row(384), row(384), row(128), row(384),
                row(128), row(128), row(128), row(128), row(128), row(128), row(384), ple,
                full((D, D)), full((D, D)), full((PLE_DIM, D)), full((8, CONV_W)), full((1, CONV_W)),
                full((3, LANES, NSA_W)), full((1, D))]
    return pl.pallas_call(
        functools.partial(_out_body, tm=tm, final=final),
        grid=(B, T // tm),
        in_specs=in_specs,
        out_specs=row(D),
        out_shape=jax.ShapeDtypeStruct((B, T, D), F32),
        compiler_params=_cparams(("arbitrary", "arbitrary")),
        name="out_proj",
    )(h, conv, conv, oc, os_, ow, ng, nz, *outs, *lses, dz, p, wout, wgate, wproj, cw, cb,
      _gate_expand(), nf)


def kernel(x, p, norm_mix, w_in, conv_w, conv_b, cmp_pe, cmp_w1, cmp_w2, w_out, w_ple_gate, w_ple_proj,
           norm_final):
    B, T, D = x.shape
    depth = w_in.shape[0]
    assert D == D_MODEL and T % 2048 == 0 and T // SLC_BLOCK <= N_SLC_PAD
    h = x
    for i in range(depth):
        w = _permute_w_in(w_in[i])
        (conv, qa, kaug, vspT, kwp, vwT, kc, vc, ng, nz, dq, dk, dv, dz) = _inproj(
            h, norm_mix[i].reshape(1, D), w)
        kcmp, vcmpT = _compress(kc, vc, *_compress_weights(cmp_pe[i], cmp_w1[i], cmp_w2[i]))
        oc, selb = _cmp_select(qa, kcmp, vcmpT)
        os_ = _slc_attn(qa, selb, kaug, vspT)
        ow = _win_attn(qa, kwp, vwT)
        outs, lses = _dilated(dq, dk, dv)
        cw = jnp.zeros((8, CONV_W), F32).at[0:CONV_K].set(conv_w[i])
        h = _out_proj(h, conv, oc, os_, ow, ng, nz, outs, lses, dz, p, i,
                      w_out[i].astype(BF16), w_ple_gate[i].astype(BF16), w_ple_proj[i].astype(BF16),
                      cw, conv_b[i].reshape(1, CONV_W), norm_final.reshape(1, D), final=(i == depth - 1))
    return h
```

```python
import functools

import numpy as np
import jax
import jax.numpy as jnp
from jax import lax
from jax.experimental import pallas as pl
from jax.experimental.pallas import tpu as pltpu

F32 = jnp.float32
BF16 = jnp.bfloat16

D_MODEL = 1024
PLE_DIM = 256
HEAD_DIM = 64
RMS_EPS = 1e-6
CONV_W = 256
CONV_K = 3
NSA_HEADS = 6
NSA_KV_HEADS = 2
NSA_GROUP = 3
NSA_W = 384
CMP_BLOCK = 32
CMP_STRIDE = 16
SLC_BLOCK = 64
SLC_TOPK = 16
WIN = 512
FORCE_SCORE = 1e4
DIL_PAIRS = ((128, 1), (512, 4), (2048, 16))
DIL_W = 384
DIL_BLOCK = 128
IN_SIZES = (256,) * 4 + (384,) + (128,) * 6 + (18, 384) + (384,) * 4

LANES = 128
MXU_COLS = 256
N_SLC_PAD = 128
Q_TILE = 128
SLC_Q_TILE = 256
CMP_Q_TILE = 512
CMP_CLASSES = 4
WIN_Q_TILE = 256
V_ROWS = 80
KV_TILE = 256
ROW_TILE = 512
NEG = -1e30
SCALE = HEAD_DIM ** -0.5
LOG2E = float(np.log2(np.e))
VMEM_LIMIT = 56 * 1024 * 1024

_NT = (((1,), (1,)), ((), ()))


def _cparams(sem):
    return pltpu.CompilerParams(dimension_semantics=sem, vmem_limit_bytes=VMEM_LIMIT)


_C_CONV = 0
_C_NQ = 1024
_C_KV = _C_NQ + 384
_C_KC = _C_KV + 512
_C_VC = _C_KC + 128
_C_NG = _C_VC + 128
_C_NZ = _C_NG + 128
_C_DQ = _C_NZ + 384
_C_END = _C_DQ + 4 * 384


def _permute_w_in(w):
    offs = np.concatenate([[0], np.cumsum(IN_SIZES)]).astype(int)
    o = lambda k: int(offs[k])
    nq = w[:, o(4):o(5)] * (SCALE * LOG2E)
    dq = w[:, o(13):o(14)] * SCALE
    pad = jnp.zeros((w.shape[0], LANES - 18), w.dtype)
    out = jnp.concatenate([w[:, o(0):o(4)], nq, w[:, o(7):o(11)], w[:, o(5):o(7)], w[:, o(11):o(12)], pad,
                           w[:, o(12):o(13)], dq, w[:, o(14):o(17)]], axis=1)
    assert out.shape[1] == _C_END
    return out.astype(BF16)


def _head_lanes(c, lane_lo, odd, fill):
    if odd:
        c = pltpu.roll(c, HEAD_DIM, 1)
    return jnp.where(lane_lo, c, fill)


def _inproj_body(x_ref, g_ref, w_ref, conv_ref, qa_ref, kaug_ref, vspT_ref, kwp_ref, vwT_ref,
                 kc_ref, vc_ref, ng_ref, nz_ref, dq_ref, dk_ref, dv_ref, dz_ref, *, tm):
    x = x_ref[0]
    ms = jnp.mean(x * x, axis=-1, keepdims=True)
    xn = (x * lax.rsqrt(ms + RMS_EPS) * g_ref[...]).astype(BF16)

    def seg(a, b):
        return jnp.dot(xn, w_ref[:, a:b], preferred_element_type=F32)

    conv_ref[0] = seg(_C_CONV, _C_NQ)

    t0 = pl.program_id(1) * tm
    row = lax.broadcasted_iota(jnp.int32, (tm, LANES), 0) + t0
    lane = lax.broadcasted_iota(jnp.int32, (tm, LANES), 1)
    onehot = jnp.where((row >> 6) == lane, 1.0, 0.0).astype(BF16)
    lane_lo = lane < HEAD_DIM

    q = seg(_C_NQ, _C_KV)
    for h in range(NSA_HEADS):
        c = q[:, (h // 2) * LANES:(h // 2 + 1) * LANES]
        qa_ref[0, h] = _head_lanes(c, lane_lo, h % 2 == 1, 0.0).astype(BF16)

    kv = seg(_C_KV, _C_KC)
    for g in range(NSA_KV_HEADS):
        ks = _head_lanes(kv[:, 0:128], lane_lo, g == 1, 0.0).astype(BF16)
        kaug_ref[0, g] = jnp.concatenate([onehot, ks], axis=1)
        vspT_ref[0, g] = _head_lanes(kv[:, 128:256], lane_lo, g == 1, 1.0).T[0:V_ROWS, :].astype(BF16)
        kwp_ref[0, g] = _head_lanes(kv[:, 256:384], lane_lo, g == 1, 0.0).astype(BF16)
        vwT_ref[0, g] = _head_lanes(kv[:, 384:512], lane_lo, g == 1, 1.0).T[0:V_ROWS, :].astype(BF16)

    misc = seg(_C_KC, _C_DQ)
    kc_ref[0] = misc[:, 0:128]
    vc_ref[0] = misc[:, 128:256]
    ng_ref[0] = misc[:, 256:384]
    nz_ref[0] = misc[:, 384:768]
    dil = seg(_C_DQ, _C_END)
    dq_ref[0] = dil[:, 0:384]
    dk_ref[0] = dil[:, 384:768]
    dv_ref[0] = dil[:, 768:1152]
    dz_ref[0] = dil[:, 1152:1536]


def _inproj(h, g, w):
    B, T, D = h.shape
    tm = min(ROW_TILE, T)
    assert T % tm == 0
    row = lambda n: pl.BlockSpec((1, tm, n), lambda b, i: (b, i, 0))
    head = lambda nh, n: pl.BlockSpec((1, nh, tm, n), lambda b, i: (b, 0, i, 0))
    sd = jax.ShapeDtypeStruct
    out_shape = (
        sd((B, T, 1024), F32), sd((B, 6, T, LANES), BF16), sd((B, 2, T, 2 * LANES), BF16),
        sd((B, 2, V_ROWS, T), BF16), sd((B, 2, T, LANES), BF16), sd((B, 2, V_ROWS, T), BF16),
        sd((B, T, 128), F32), sd((B, T, 128), F32), sd((B, T, 128), F32), sd((B, T, 384), F32),
        sd((B, T, 384), F32), sd((B, T, 384), F32), sd((B, T, 384), F32), sd((B, T, 384), F32))
    headT = pl.BlockSpec((1, 2, V_ROWS, tm), lambda b, i: (b, 0, 0, i))
    out_specs = (row(1024), head(6, LANES), head(2, 2 * LANES), headT, head(2, LANES),
                 headT, row(128), row(128), row(128), row(384),
                 row(384), row(384), row(384), row(384))
    return pl.pallas_call(
        functools.partial(_inproj_body, tm=tm),
        grid=(B, T // tm),
        in_specs=[row(D), pl.BlockSpec((1, D), lambda b, i: (0, 0)),
                  pl.BlockSpec((D, _C_END), lambda b, i: (0, 0))],
        out_specs=out_specs, out_shape=out_shape,
        compiler_params=_cparams(("arbitrary", "arbitrary")),
        name="inproj",
    )(h, g, w)


def _compress_weights(pe, w1, w2):
    eye = jnp.eye(NSA_KV_HEADS, dtype=F32)
    pe4 = pe.reshape(2, 2, 16, 1, HEAD_DIM)
    pe_t = jnp.broadcast_to(pe4, (2, 2, 16, NSA_KV_HEADS, HEAD_DIM)).reshape(2, 2, 16, 1, 128)
    w1r = w1.reshape(2, 2, 16, HEAD_DIM, HEAD_DIM)
    w1b = jnp.einsum("khmeo,gf->khmgefo", w1r, eye).reshape(2, 2, 16, 128, 128)
    w2k = jnp.zeros((NSA_KV_HEADS, 128, 128), F32)
    for g in range(NSA_KV_HEADS):
        w2k = w2k.at[g, g * 64:(g + 1) * 64, 0:64].set(w2[0])
    w2v = jnp.einsum("eo,gf->gefo", w2[1], eye).reshape(128, 128)
    return pe_t.astype(F32), w1b.astype(BF16), w2k.astype(BF16), w2v.astype(BF16)


def _compress_body(kc_ref, vc_ref, pe_ref, w1_ref, w2k_ref, w2v_ref, kcmp_ref, vcmpT_ref, *, nc):
    def hidden(x_ref, kv):
        a0 = jnp.zeros((nc, LANES), F32)
        a1 = jnp.zeros((nc, LANES), F32)
        for m in range(CMP_STRIDE):
            z = x_ref[0, pl.ds(m, nc, stride=CMP_STRIDE), :]
            a0 = a0 + jnp.dot((z + pe_ref[kv, 0, m]).astype(BF16), w1_ref[kv, 0, m],
                              preferred_element_type=F32)
            a1 = a1 + jnp.dot((z + pe_ref[kv, 1, m]).astype(BF16), w1_ref[kv, 1, m],
                              preferred_element_type=F32)
        hid = a0 + pltpu.roll(a1, nc - 1, 0)
        return jax.nn.silu(hid).astype(BF16)

    hk = hidden(kc_ref, 0)
    for g in range(NSA_KV_HEADS):
        kcmp_ref[0, g] = jnp.dot(hk, w2k_ref[g], preferred_element_type=F32).astype(BF16)
    hv = hidden(vc_ref, 1)
    vout = jnp.dot(hv, w2v_ref[...], preferred_element_type=F32)
    vcmpT_ref[0] = vout.T.astype(BF16)


def _compress(kc, vc, pe_t, w1b, w2k, w2v):
    B, T, _ = kc.shape
    nc = T // CMP_STRIDE
    full = lambda shape: pl.BlockSpec(shape, lambda b: (0,) * len(shape))
    return pl.pallas_call(
        functools.partial(_compress_body, nc=nc),
        grid=(B,),
        in_specs=[pl.BlockSpec((1, T, LANES), lambda b: (b, 0, 0)),
                  pl.BlockSpec((1, T, LANES), lambda b: (b, 0, 0)),
                  full((2, 2, 16, 1, 128)), full((2, 2, 16, 128, 128)), full((2, 128, 128)), full((128, 128))],
        out_specs=(pl.BlockSpec((1, 2, nc, 128), lambda b: (b, 0, 0, 0)),
                   pl.BlockSpec((1, 128, nc), lambda b: (b, 0, 0))),
        out_shape=(jax.ShapeDtypeStruct((B, 2, nc, 128), BF16),
                   jax.ShapeDtypeStruct((B, 128, nc), BF16)),
        compiler_params=_cparams(("arbitrary",)),
        name="nsa_compress",
    )(kc, vc, pe_t, w1b, w2k, w2v)


def _split_bf16(x):
    hi = x.astype(BF16)
    lo = (x - hi.astype(F32)).astype(BF16)
    return hi, lo


def _cmp_select_body(qa_ref, kcmp_ref, vcmpT_ref, ovT_ref, oc_ref, selb_ref, *, nc, qt, n_tiles_static):
    i = pl.program_id(1)
    q0 = i * qt

    def run(nrow, nblk):
        jj = lax.broadcasted_iota(jnp.int32, (nrow, qt), 0)
        tt = lax.broadcasted_iota(jnp.int32, (nrow, qt), 1) + q0
        cmask = (jj * CMP_STRIDE + (CMP_BLOCK - 1)) <= tt
        blk = lax.broadcasted_iota(jnp.int32, (nblk, qt), 0)
        ts = lax.broadcasted_iota(jnp.int32, (nblk, qt), 1) + q0
        cur = ts >> 6
        forced = (blk == 0) | (blk == cur) | (blk == cur - 1)
        causal = blk * SLC_BLOCK <= ts
        blkf = blk.astype(F32)
        ovT = ovT_ref[0:nblk, 0:nrow]

        raw = [lax.dot_general(kcmp_ref[0, h // NSA_GROUP, 0:nrow, :], qa_ref[0, h], _NT,
                               preferred_element_type=F32) for h in range(NSA_HEADS)]
        ocT = []
        for g in range(NSA_KV_HEADS):
            psum = jnp.zeros((nrow, qt), F32)
            for r in range(NSA_GROUP):
                h = g * NSA_GROUP + r
                s = jnp.where(cmask, raw[h], -jnp.inf)
                m = jnp.max(s, axis=0, keepdims=True)
                m = jnp.where(m > -jnp.inf, m, 0.0)
                e = jnp.exp2(s - m)
                den = jnp.sum(e, axis=0, keepdims=True)
                rden = 1.0 / jnp.maximum(den, 1e-30)
                ocT.append(jnp.dot(vcmpT_ref[0, g * 64:(g + 1) * 64, 0:nrow], e.astype(BF16),
                                   preferred_element_type=F32) * rden)
                psum = psum + e * rden
            hi, lo = _split_bf16(psum)
            imp = (jnp.dot(ovT, hi, preferred_element_type=F32)
                   + jnp.dot(ovT, lo, preferred_element_type=F32))
            imp = jnp.where(causal, imp, -1.0)
            imp = jnp.where(forced, -2.0, imp)
            sel = jnp.where(forced, 1.0, 0.0)
            for _ in range(SLC_TOPK - 3):
                mx = jnp.max(imp, axis=0, keepdims=True)
                first = jnp.min(jnp.where(imp == mx, blkf, float(N_SLC_PAD)), axis=0, keepdims=True)
                hit = blkf == first
                sel = jnp.where(hit, 1.0, sel)
                imp = jnp.where(hit, -2.0, imp)
            bias = jnp.where(sel > 0.5, 0.0, NEG)
            if nblk < N_SLC_PAD:
                bias = jnp.concatenate([bias, jnp.full((N_SLC_PAD - nblk, qt), NEG, F32)], axis=0)
            for sub in range(qt // Q_TILE):
                selb_ref[0, g, sub * Q_TILE:(sub + 1) * Q_TILE, :] = (
                    bias[:, sub * Q_TILE:(sub + 1) * Q_TILE].T.astype(BF16))
        for pr in range(NSA_HEADS // 2):
            pair = jnp.concatenate([ocT[2 * pr], ocT[2 * pr + 1]], axis=0)
            for sub in range(qt // Q_TILE):
                oc_ref[0, sub * Q_TILE:(sub + 1) * Q_TILE, pr * LANES:(pr + 1) * LANES] = (
                    pair[:, sub * Q_TILE:(sub + 1) * Q_TILE].T.astype(oc_ref.dtype))

    bounds = [(k * n_tiles_static) // CMP_CLASSES for k in range(CMP_CLASSES + 1)]
    for k in range(CMP_CLASSES):
        lo, hi = bounds[k], bounds[k + 1]
        if hi == lo:
            continue
        nrow = min(nc, -(-(hi * qt // CMP_STRIDE) // LANES) * LANES)
        nblk = min(N_SLC_PAD, max(32, -(-(hi * qt // SLC_BLOCK) // 8) * 8))
        pl.when((i >= lo) & (i < hi))(functools.partial(run, nrow, nblk))


def _overlap_T(nc):
    j = np.arange(nc)[None, :] * CMP_STRIDE
    s = np.arange(N_SLC_PAD)[:, None] * SLC_BLOCK
    ov = (j < s + SLC_BLOCK) & (j + CMP_BLOCK > s) & (np.arange(nc)[None, :] < nc - 1)
    return jnp.asarray(ov.astype(np.float32)).astype(BF16)


def _cmp_select(qa, kcmp, vcmpT):
    B, _, T, _ = qa.shape
    nc = T // CMP_STRIDE
    qt = CMP_Q_TILE
    return pl.pallas_call(
        functools.partial(_cmp_select_body, nc=nc, qt=qt, n_tiles_static=T // qt),
        grid=(B, T // qt),
        in_specs=[pl.BlockSpec((1, 6, qt, LANES), lambda b, i: (b, 0, i, 0)),
                  pl.BlockSpec((1, 2, nc, LANES), lambda b, i: (b, 0, 0, 0)),
                  pl.BlockSpec((1, LANES, nc), lambda b, i: (b, 0, 0)),
                  pl.BlockSpec((N_SLC_PAD, nc), lambda b, i: (0, 0))],
        out_specs=(pl.BlockSpec((1, qt, NSA_W), lambda b, i: (b, i, 0)),
                   pl.BlockSpec((1, 2, qt, N_SLC_PAD), lambda b, i: (b, 0, i, 0))),
        out_shape=(jax.ShapeDtypeStruct((B, T, NSA_W), BF16),
                   jax.ShapeDtypeStruct((B, 2, T, N_SLC_PAD), BF16)),
        compiler_params=_cparams(("arbitrary", "arbitrary")),
        name="nsa_cmp_select",
    )(qa, kcmp, vcmpT, _overlap_T(nc))


M_INIT = -1e29


def _store_heads_T(acc_ref, o_ref, qs):
    for sub in range(qs // Q_TILE):
        outs = []
        for g in range(NSA_KV_HEADS):
            for r in range(NSA_GROUP):
                lo = r * qs + sub * Q_TILE
                a = acc_ref[g, :, lo:lo + Q_TILE]
                outs.append(a[0:HEAD_DIM, :] / a[HEAD_DIM:HEAD_DIM + 1, :])
        for pr in range(NSA_HEADS // 2):
            pair = jnp.concatenate([outs[2 * pr], outs[2 * pr + 1]], axis=0)
            o_ref[0, sub * Q_TILE:(sub + 1) * Q_TILE, pr * LANES:(pr + 1) * LANES] = pair.T.astype(o_ref.dtype)


def _slc_body(qa_ref, selb_ref, kaug_ref, vspT_ref, o_ref, acc_ref, s_ref, p_ref, al_ref, qT_ref, *, qs):
    i = pl.program_id(1)
    q0 = i * qs
    rows = NSA_GROUP * qs
    nfull = q0 // KV_TILE
    t_len = kaug_ref.shape[2]
    for g in range(NSA_KV_HEADS):
        bias = selb_ref[0, g]
        for r in range(NSA_GROUP):
            qb = jnp.concatenate([bias, qa_ref[0, g * NSA_GROUP + r]], axis=1)
            for c in range(qs // MXU_COLS):
                lo = c * MXU_COLS
                qT_ref[g, r, :, lo:lo + MXU_COLS] = qb[lo:lo + MXU_COLS, :].T
    acc_ref[...] = jnp.zeros(acc_ref.shape, F32)
    s_ref[...] = jnp.full(s_ref.shape, NEG, F32)
    p_ref[...] = jnp.zeros(p_ref.shape, BF16)
    al_ref[...] = jnp.ones(al_ref.shape, F32)
    kpos = lax.broadcasted_iota(jnp.int32, (KV_TILE, MXU_COLS), 0)
    tq = lax.broadcasted_iota(jnp.int32, (KV_TILE, MXU_COLS), 1) + q0

    def trip(j, par, ms, qk, softmax):
        st_c = pl.multiple_of(jnp.maximum(j - 2, 0) * KV_TILE, KV_TILE)
        st_a = pl.multiple_of(jnp.minimum(j * KV_TILE, t_len - KV_TILE), KV_TILE)
        out = []
        for g in range(NSA_KV_HEADS):
            if qk is not None:
                kt = kaug_ref[0, g, pl.ds(st_a, KV_TILE), :]
                for r in range(NSA_GROUP):
                    for c in range(qs // MXU_COLS):
                        lo = c * MXU_COLS
                        s = jnp.dot(kt, qT_ref[g, r, :, lo:lo + MXU_COLS], preferred_element_type=F32)
                        if qk == "masked":
                            s = jnp.where(kpos + j * KV_TILE <= tq + lo, s, NEG)
                        s_ref[par, g, :, r * qs + lo:r * qs + lo + MXU_COLS] = s
        for g in range(NSA_KV_HEADS):
            vt = vspT_ref[0, g, :, pl.ds(st_c, KV_TILE)]
            for c in range(rows // MXU_COLS):
                cols = slice(c * MXU_COLS, (c + 1) * MXU_COLS)
                acc_ref[g, :, cols] = al_ref[par, g, :, cols] * acc_ref[g, :, cols] + jnp.dot(
                    vt, p_ref[par, g, :, cols], preferred_element_type=F32)
        for g in range(NSA_KV_HEADS):
            m_new = ms[g]
            if softmax:
                s = s_ref[1 - par, g]
                m_new = jnp.maximum(ms[g], jnp.max(s, axis=0, keepdims=True))
                al_ref[1 - par, g] = jnp.exp2(ms[g] - m_new)
                p_ref[1 - par, g] = jnp.exp2(s - m_new).astype(BF16)
            out.append(m_new)
        return tuple(out)

    def pair(k, ms):
        ms = trip(2 * k, 0, ms, "plain", True)
        return trip(2 * k + 1, 1, ms, "plain", True)

    def quad(k, ms):
        return pair(2 * k + 1, pair(2 * k, ms))

    npairs = nfull // 2
    nquads = npairs // 2
    ms = tuple(jnp.full((1, rows), M_INIT, F32) for _ in range(NSA_KV_HEADS))
    ms = lax.fori_loop(0, nquads, quad, ms)
    ms = lax.fori_loop(2 * nquads, npairs, pair, ms)
    t0 = 2 * npairs
    nmask = max(2, qs // KV_TILE)
    for d in range(nmask):
        ms = trip(t0 + d, d % 2, ms, "masked", True)
    ms = trip(t0 + nmask, nmask % 2, ms, None, True)
    trip(t0 + nmask + 1, (nmask + 1) % 2, ms, None, False)
    _store_heads_T(acc_ref, o_ref, qs)


def _slc_attn(qa, selb, kaug, vspT, qs=SLC_Q_TILE):
    B, _, T, _ = qa.shape
    assert T % qs == 0 and KV_TILE % qs == 0
    rows = NSA_GROUP * qs
    return pl.pallas_call(
        functools.partial(_slc_body, qs=qs),
        grid=(B, T // qs),
        in_specs=[pl.BlockSpec((1, 6, qs, LANES), lambda b, i: (b, 0, i, 0)),
                  pl.BlockSpec((1, 2, qs, N_SLC_PAD), lambda b, i: (b, 0, i, 0)),
                  pl.BlockSpec((1, 2, T, 2 * LANES), lambda b, i: (b, 0, 0, 0)),
                  pl.BlockSpec((1, 2, V_ROWS, T), lambda b, i: (b, 0, 0, 0))],
        out_specs=pl.BlockSpec((1, qs, NSA_W), lambda b, i: (b, i, 0)),
        out_shape=jax.ShapeDtypeStruct((B, T, NSA_W), BF16),
        scratch_shapes=[pltpu.VMEM((NSA_KV_HEADS, V_ROWS, rows), F32),
                        pltpu.VMEM((2, NSA_KV_HEADS, KV_TILE, rows), F32),
                        pltpu.VMEM((2, NSA_KV_HEADS, KV_TILE, rows), BF16),
                        pltpu.VMEM((2, NSA_KV_HEADS, 1, rows), F32),
                        pltpu.VMEM((NSA_KV_HEADS, NSA_GROUP, 2 * LANES, qs), BF16)],
        compiler_params=_cparams(("arbitrary", "arbitrary")),
        name="nsa_slc_attn",
    )(qa, selb, kaug, vspT)


def _win_body(qa_ref, kwp_ref, vwT_ref, o_ref, acc_ref, *, qs):
    i = pl.program_id(1)
    q0 = i * qs
    rows = NSA_GROUP * qs
    ntile = WIN // qs + 1
    kk = lax.broadcasted_iota(jnp.int32, (qs, rows), 0)
    tq = (lax.broadcasted_iota(jnp.int32, (qs, rows), 1) & (qs - 1)) + q0
    starts = [pl.multiple_of(jnp.maximum(i - (ntile - 1) + d, 0) * qs, qs) for d in range(ntile)]
    raw = []
    for g in range(NSA_KV_HEADS):
        qrows = jnp.concatenate([qa_ref[0, g * NSA_GROUP + r] for r in range(NSA_GROUP)], axis=0)
        raw.append([lax.dot_general(kwp_ref[0, g, pl.ds(starts[d], qs), :], qrows, _NT,
                                    preferred_element_type=F32) for d in range(ntile)])
    for g in range(NSA_KV_HEADS):
        scores = []
        for d in range(ntile):
            kpos = kk + (i - (ntile - 1) + d) * qs
            if d == 0:
                mask = (kpos >= 0) & ((tq - kpos) < WIN)
            elif d == ntile - 1:
                mask = kpos <= tq
            else:
                mask = kpos >= 0
            scores.append(jnp.where(mask, raw[g][d], NEG))
        m = functools.reduce(jnp.maximum, [jnp.max(s, axis=0, keepdims=True) for s in scores])
        acc = jnp.zeros((V_ROWS, rows), F32)
        for s, st in zip(scores, starts):
            acc = acc + jnp.dot(vwT_ref[0, g, :, pl.ds(st, qs)], jnp.exp2(s - m).astype(BF16),
                                preferred_element_type=F32)
        acc_ref[g] = acc
    _store_heads_T(acc_ref, o_ref, qs)


def _win_attn(qa, kwp, vwT, qs=WIN_Q_TILE):
    B, _, T, _ = qa.shape
    assert WIN % qs == 0 and T % qs == 0
    rows = NSA_GROUP * qs
    return pl.pallas_call(
        functools.partial(_win_body, qs=qs),
        grid=(B, T // qs),
        in_specs=[pl.BlockSpec((1, 6, qs, LANES), lambda b, i: (b, 0, i, 0)),
                  pl.BlockSpec((1, 2, T, LANES), lambda b, i: (b, 0, 0, 0)),
                  pl.BlockSpec((1, 2, V_ROWS, T), lambda b, i: (b, 0, 0, 0))],
        out_specs=pl.BlockSpec((1, qs, NSA_W), lambda b, i: (b, i, 0)),
        out_shape=jax.ShapeDtypeStruct((B, T, NSA_W), BF16),
        scratch_shapes=[pltpu.VMEM((NSA_KV_HEADS, V_ROWS, rows), F32)],
        compiler_params=_cparams(("arbitrary", "arbitrary")),
        name="nsa_win_attn",
    )(qa, kwp, vwT)


DIL_CHUNK = 2048
DIL_UNROLL = 4


def _dil_body(q_ref, kp_ref, kc_ref, vp_ref, vc_ref, o_ref, l_ref, kbuf, vbuf, *, dil):
    i = pl.program_id(1)
    span = DIL_BLOCK * dil
    nsb = DIL_CHUNK // span
    kbuf[0:span, :] = kp_ref[0]
    kbuf[span:span + DIL_CHUNK, :] = kc_ref[0]
    vbuf[0:span, :] = vp_ref[0]
    vbuf[span:span + DIL_CHUNK, :] = vc_ref[0]
    a = lax.broadcasted_iota(jnp.int32, (DIL_BLOCK, 2 * DIL_BLOCK), 0)
    c = lax.broadcasted_iota(jnp.int32, (DIL_BLOCK, 2 * DIL_BLOCK), 1)
    dist = a + DIL_BLOCK - c
    band = (dist >= 0) & (dist <= DIL_BLOCK)
    head0 = lax.broadcasted_iota(jnp.int32, (DIL_BLOCK, LANES), 1) < HEAD_DIM
    shift = dil.bit_length() - 1

    def trip(t, carry):
        work = []
        for k in range(DIL_UNROLL):
            u = t * DIL_UNROLL + k
            sb = u >> shift
            r = u & (dil - 1)
            row0 = sb * span + r
            if dil == 1:
                rows_q = pl.ds(row0, DIL_BLOCK)
                rows_kv = pl.ds(row0, 2 * DIL_BLOCK)
            else:
                rows_q = pl.ds(row0, DIL_BLOCK, stride=dil)
                rows_kv = pl.ds(row0, 2 * DIL_BLOCK, stride=dil)
            q = q_ref[0, rows_q, :]
            kk = kbuf[rows_kv, :].astype(BF16)
            scs = [lax.dot_general(jnp.where(hsel, q, 0.0).astype(BF16), kk, _NT, preferred_element_type=F32)
                   for hsel in (head0, jnp.logical_not(head0))]
            work.append((sb, rows_q, rows_kv, scs))
        for sb, rows_q, rows_kv, scs in work:
            vv = vbuf[rows_kv, :].astype(BF16)
            mask = band & ((c + (i * nsb + sb - 1) * DIL_BLOCK) >= 0)
            o_h, l_h = [], []
            for sc in scs:
                sc = jnp.where(mask, sc, NEG)
                m = jnp.max(sc, axis=1, keepdims=True)
                e = jnp.exp(sc - m)
                den = jnp.sum(e, axis=1, keepdims=True)
                pv = jnp.dot(e.astype(BF16), vv, preferred_element_type=F32)
                o_h.append(pv / den)
                l_h.append(jnp.broadcast_to(m + jnp.log(den), (DIL_BLOCK, LANES)))
            o_ref[0, rows_q, :] = jnp.where(head0, o_h[0], o_h[1])
            l_ref[0, rows_q, :] = jnp.where(head0, l_h[0], l_h[1])
        return carry

    lax.fori_loop(0, DIL_CHUNK // DIL_BLOCK // DIL_UNROLL, trip, 0)


def _dil_attn(dq, dk, dv, gi, dil):
    B, T, _ = dq.shape
    span = DIL_BLOCK * dil
    assert T % DIL_CHUNK == 0 and DIL_CHUNK % span == 0
    cur = pl.BlockSpec((1, DIL_CHUNK, LANES), lambda b, i: (b, i, gi))
    prev = pl.BlockSpec((1, span, LANES), lambda b, i: (b, jnp.maximum(i * (DIL_CHUNK // span) - 1, 0), gi))
    out = pl.BlockSpec((1, DIL_CHUNK, LANES), lambda b, i: (b, i, 0))
    return pl.pallas_call(
        functools.partial(_dil_body, dil=dil),
        grid=(B, T // DIL_CHUNK),
        in_specs=[cur, prev, cur, prev, cur],
        out_specs=(out, out),
        out_shape=(jax.ShapeDtypeStruct((B, T, LANES), F32), jax.ShapeDtypeStruct((B, T, LANES), F32)),
        scratch_shapes=[pltpu.VMEM((span + DIL_CHUNK, LANES), F32), pltpu.VMEM((span + DIL_CHUNK, LANES), F32)],
        compiler_params=_cparams(("arbitrary", "arbitrary")),
        name="dilated_attn",
    )(dq, dk, dk, dv, dv)


def _dilated(dq, dk, dv):
    outs, lses = [], []
    for gi, (window, dil) in enumerate(DIL_PAIRS):
        assert window == DIL_BLOCK * dil
        o, l = _dil_attn(dq, dk, dv, gi, dil)
        outs.append(o)
        lses.append(l)
    return outs, lses


def _gate_expand():
    m = np.zeros((3, 2 * LANES, NSA_W), np.float32)
    for j in range(3):
        for h in range(NSA_HEADS):
            m[j, h * 3 + j, h * HEAD_DIM:(h + 1) * HEAD_DIM] = 1.0
            m[j, LANES + h * 3 + j, h * HEAD_DIM:(h + 1) * HEAD_DIM] = 1.0
    return jnp.asarray(m).astype(BF16)


def _out_body(h_ref, conv_ref, halo_ref, oc_ref, os_ref, ow_ref, ng_ref, nz_ref,
              o0_ref, o1_ref, o2_ref, l0_ref, l1_ref, l2_ref, dz_ref, p_ref,
              wout_ref, wgate_ref, wproj_ref, cw_ref, cb_ref, gexp_ref, nf_ref, out_ref, *, tm, final):
    i = pl.program_id(1)
    cv = conv_ref[0]
    a_b, a_c, a_h, a_z = cv[:, 0:256], cv[:, 256:512], cv[:, 512:768], cv[:, 768:1024]
    u = a_c * a_h
    hl = halo_ref[0]
    uh = hl[:, 256:512] * hl[:, 512:768] * (i > 0).astype(F32)
    rowi = lax.broadcasted_iota(jnp.int32, (tm, CONV_W), 0)
    u1 = jnp.where(rowi == 0, uh[7:8, :], pltpu.roll(u, 1, 0))
    u2 = jnp.where(rowi == 0, uh[6:7, :], jnp.where(rowi == 1, uh[7:8, :], pltpu.roll(u, 2, 0)))
    cw = cw_ref[...]
    conv = cw[0:1, :] * u2 + cw[1:2, :] * u1 + cw[2:3, :] * u + cb_ref[...]
    y_a = a_b * conv * jax.nn.silu(a_z)
    gates = jax.nn.sigmoid(ng_ref[0])
    ghl = jnp.concatenate(_split_bf16(gates), axis=1)
    y_b = jnp.zeros((tm, NSA_W), F32)
    for j, o_ref in enumerate((oc_ref, os_ref, ow_ref)):
        gj = jnp.dot(ghl, gexp_ref[j], preferred_element_type=F32)
        y_b = y_b + gj * o_ref[0].astype(F32)
    y_b = y_b * jax.nn.silu(nz_ref[0])
    l0, l1, l2 = l0_ref[0], l1_ref[0], l2_ref[0]
    mx = jnp.maximum(jnp.maximum(l0, l1), l2)
    e0, e1, e2 = jnp.exp(l0 - mx), jnp.exp(l1 - mx), jnp.exp(l2 - mx)
    den = e0 + e1 + e2
    y_c = jnp.concatenate([(e0 / den) * o0_ref[0], (e1 / den) * o1_ref[0], (e2 / den) * o2_ref[0]], axis=1)
    y_c = y_c * jax.nn.silu(dz_ref[0])
    y = jnp.concatenate([y_a, y_b, y_c], axis=1).astype(BF16)
    h1 = h_ref[0] + jnp.dot(y, wout_ref[...], preferred_element_type=F32)
    gate = jax.nn.sigmoid(jnp.dot(h1.astype(BF16), wgate_ref[...], preferred_element_type=F32))
    pp = jnp.dot(p_ref[0, 0].astype(BF16), wproj_ref[...], preferred_element_type=F32)
    h2 = h1 + gate * pp
    if final:
        ms = jnp.mean(h2 * h2, axis=-1, keepdims=True)
        h2 = h2 * lax.rsqrt(ms + RMS_EPS) * nf_ref[...]
    out_ref[0] = h2


def _out_proj(h, conv, oc, os_, ow, ng, nz, outs, lses, dz, p, layer, wout, wgate, wproj, cw, cb, nf, final):
    B, T, D = h.shape
    tm = min(ROW_TILE, T)
    row = lambda n: pl.BlockSpec((1, tm, n), lambda b, i: (b, i, 0))
    full = lambda shape: pl.BlockSpec(shape, lambda b, i: (0,) * len(shape))
    halo = pl.BlockSpec((1, 8, 1024), lambda b, i: (b, jnp.maximum(i * (tm // 8) - 1, 0), 0))
    ple = pl.BlockSpec((1, 1, tm, PLE_DIM), lambda b, i: (layer, b, i, 0))
    in_specs = [row(D), row(1024), halo, row(384), row(384), row(384), row(128), row(384),
                row(128), row(128), row(128), row(128), row(128), row(128), row(384), ple,
                full((D, D)), full((D, D)), full((PLE_DIM, D)), full((8, CONV_W)), full((1, CONV_W)),
                full((3, 2 * LANES, NSA_W)), full((1, D))]
    return pl.pallas_call(
        functools.partial(_out_body, tm=tm, final=final),
        grid=(B, T // tm),
        in_specs=in_specs,
        out_specs=row(D),
        out_shape=jax.ShapeDtypeStruct((B, T, D), F32),
        compiler_params=_cparams(("arbitrary", "arbitrary")),
        name="out_proj",
    )(h, conv, conv, oc, os_, ow, ng, nz, *outs, *lses, dz, p, wout, wgate, wproj, cw, cb,
      _gate_expand(), nf)


def kernel(x, p, norm_mix, w_in, conv_w, conv_b, cmp_pe, cmp_w1, cmp_w2, w_out, w_ple_gate, w_ple_proj,
           norm_final):
    B, T, D = x.shape
    depth = w_in.shape[0]
    assert D == D_MODEL and T % 2048 == 0 and T // SLC_BLOCK <= N_SLC_PAD
    h = x
    for i in range(depth):
        w = _permute_w_in(w_in[i])
        (conv, qa, kaug, vspT, kwp, vwT, kc, vc, ng, nz, dq, dk, dv, dz) = _inproj(
            h, norm_mix[i].reshape(1, D), w)
        kcmp, vcmpT = _compress(kc, vc, *_compress_weights(cmp_pe[i], cmp_w1[i], cmp_w2[i]))
        oc, selb = _cmp_select(qa, kcmp, vcmpT)
        os_ = _slc_attn(qa, selb, kaug, vspT)
        ow = _win_attn(qa, kwp, vwT)
        outs, lses = _dilated(dq, dk, dv)
        cw = jnp.zeros((8, CONV_W), F32).at[0:CONV_K].set(conv_w[i])
        h = _out_proj(h, conv, oc, os_, ow, ng, nz, outs, lses, dz, p, i,
                      w_out[i].astype(BF16), w_ple_gate[i].astype(BF16), w_ple_proj[i].astype(BF16),
                      cw, conv_b[i].reshape(1, CONV_W), norm_final.reshape(1, D), final=(i == depth - 1))
    return h
```

```python
import functools

import numpy as np
import jax
import jax.numpy as jnp
from jax import lax
from jax.experimental import pallas as pl
from jax.experimental.pallas import tpu as pltpu

F32 = jnp.float32
BF16 = jnp.bfloat16

D_MODEL = 1024
PLE_DIM = 256
HEAD_DIM = 64
RMS_EPS = 1e-6
CONV_W = 256
CONV_K = 3
NSA_HEADS = 6
NSA_KV_HEADS = 2
NSA_GROUP = 3
NSA_W = 384
CMP_BLOCK = 32
CMP_STRIDE = 16
SLC_BLOCK = 64
SLC_TOPK = 16
WIN = 512
FORCE_SCORE = 1e4
DIL_PAIRS = ((128, 1), (512, 4), (2048, 16))
DIL_W = 384
DIL_BLOCK = 128
IN_SIZES = (256,) * 4 + (384,) + (128,) * 6 + (18, 384) + (384,) * 4

LANES = 128
MXU_COLS = 256
N_SLC_PAD = 128
Q_TILE = 128
SLC_Q_TILE = 256
CMP_Q_TILE = 512
CMP_CLASSES = 4
WIN_Q_TILE = 256
V_ROWS = 80
KV_TILE = 256
ROW_TILE = 512
NEG = -1e30
SCALE = HEAD_DIM ** -0.5
LOG2E = float(np.log2(np.e))
VMEM_LIMIT = 56 * 1024 * 1024

_NT = (((1,), (1,)), ((), ()))


def _cparams(sem):
    return pltpu.CompilerParams(dimension_semantics=sem, vmem_limit_bytes=VMEM_LIMIT)


_C_CONV = 0
_C_NQ = 1024
_C_KV = _C_NQ + 384
_C_KC = _C_KV + 512
_C_VC = _C_KC + 128
_C_NG = _C_VC + 128
_C_NZ = _C_NG + 128
_C_DQ = _C_NZ + 384
_C_END = _C_DQ + 4 * 384


def _permute_w_in(w):
    offs = np.concatenate([[0], np.cumsum(IN_SIZES)]).astype(int)
    o = lambda k: int(offs[k])
    nq = w[:, o(4):o(5)] * (SCALE * LOG2E)
    dq = w[:, o(13):o(14)] * SCALE
    pad = jnp.zeros((w.shape[0], LANES - 18), w.dtype)
    out = jnp.concatenate([w[:, o(0):o(4)], nq, w[:, o(7):o(11)], w[:, o(5):o(7)], w[:, o(11):o(12)], pad,
                           w[:, o(12):o(13)], dq, w[:, o(14):o(17)]], axis=1)
    assert out.shape[1] == _C_END
    return out.astype(BF16)


def _head_lanes(c, lane_lo, odd, fill):
    if odd:
        c = pltpu.roll(c, HEAD_DIM, 1)
    return jnp.where(lane_lo, c, fill)


def _inproj_body(x_ref, g_ref, w_ref, conv_ref, qa_ref, kaug_ref, vspT_ref, kwp_ref, vwT_ref,
                 kc_ref, vc_ref, ng_ref, nz_ref, dq_ref, dk_ref, dv_ref, dz_ref, *, tm):
    x = x_ref[0]
    ms = jnp.mean(x * x, axis=-1, keepdims=True)
    xn = (x * lax.rsqrt(ms + RMS_EPS) * g_ref[...]).astype(BF16)

    def seg(a, b):
        return jnp.dot(xn, w_ref[:, a:b], preferred_element_type=F32)

    conv_ref[0] = seg(_C_CONV, _C_NQ)

    t0 = pl.program_id(1) * tm
    row = lax.broadcasted_iota(jnp.int32, (tm, LANES), 0) + t0
    lane = lax.broadcasted_iota(jnp.int32, (tm, LANES), 1)
    onehot = jnp.where((row >> 6) == lane, 1.0, 0.0).astype(BF16)
    lane_lo = lane < HEAD_DIM

    q = seg(_C_NQ, _C_KV)
    for h in range(NSA_HEADS):
        c = q[:, (h // 2) * LANES:(h // 2 + 1) * LANES]
        qa_ref[0, h] = _head_lanes(c, lane_lo, h % 2 == 1, 0.0).astype(BF16)

    kv = seg(_C_KV, _C_KC)
    for g in range(NSA_KV_HEADS):
        ks = _head_lanes(kv[:, 0:128], lane_lo, g == 1, 0.0).astype(BF16)
        kaug_ref[0, g] = jnp.concatenate([onehot, ks], axis=1)
        vspT_ref[0, g] = _head_lanes(kv[:, 128:256], lane_lo, g == 1, 1.0).T[0:V_ROWS, :].astype(BF16)
        kwp_ref[0, g] = _head_lanes(kv[:, 256:384], lane_lo, g == 1, 0.0).astype(BF16)
        vwT_ref[0, g] = _head_lanes(kv[:, 384:512], lane_lo, g == 1, 1.0).T[0:V_ROWS, :].astype(BF16)

    misc = seg(_C_KC, _C_DQ)
    kc_ref[0] = misc[:, 0:128]
    vc_ref[0] = misc[:, 128:256]
    ng_ref[0] = misc[:, 256:384]
    nz_ref[0] = misc[:, 384:768]
    dil = seg(_C_DQ, _C_END)
    dq_ref[0] = dil[:, 0:384]
    dk_ref[0] = dil[:, 384:768]
    dv_ref[0] = dil[:, 768:1152]
    dz_ref[0] = dil[:, 1152:1536]


def _inproj(h, g, w):
    B, T, D = h.shape
    tm = min(ROW_TILE, T)
    assert T % tm == 0
    row = lambda n: pl.BlockSpec((1, tm, n), lambda b, i: (b, i, 0))
    head = lambda nh, n: pl.BlockSpec((1, nh, tm, n), lambda b, i: (b, 0, i, 0))
    sd = jax.ShapeDtypeStruct
    out_shape = (
        sd((B, T, 1024), F32), sd((B, 6, T, LANES), BF16), sd((B, 2, T, 2 * LANES), BF16),
        sd((B, 2, V_ROWS, T), BF16), sd((B, 2, T, LANES), BF16), sd((B, 2, V_ROWS, T), BF16),
        sd((B, T, 128), F32), sd((B, T, 128), F32), sd((B, T, 128), F32), sd((B, T, 384), F32),
        sd((B, T, 384), F32), sd((B, T, 384), F32), sd((B, T, 384), F32), sd((B, T, 384), F32))
    headT = pl.BlockSpec((1, 2, V_ROWS, tm), lambda b, i: (b, 0, 0, i))
    out_specs = (row(1024), head(6, LANES), head(2, 2 * LANES), headT, head(2, LANES),
                 headT, row(128), row(128), row(128), row(384),
                 row(384), row(384), row(384), row(384))
    return pl.pallas_call(
        functools.partial(_inproj_body, tm=tm),
        grid=(B, T // tm),
        in_specs=[row(D), pl.BlockSpec((1, D), lambda b, i: (0, 0)),
                  pl.BlockSpec((D, _C_END), lambda b, i: (0, 0))],
        out_specs=out_specs, out_shape=out_shape,
        compiler_params=_cparams(("arbitrary", "arbitrary")),
        name="inproj",
    )(h, g, w)


def _compress_weights(pe, w1, w2):
    eye = jnp.eye(NSA_KV_HEADS, dtype=F32)
    pe4 = pe.reshape(2, 2, 16, 1, HEAD_DIM)
    pe_t = jnp.broadcast_to(pe4, (2, 2, 16, NSA_KV_HEADS, HEAD_DIM)).reshape(2, 2, 16, 1, 128)
    w1r = w1.reshape(2, 2, 16, HEAD_DIM, HEAD_DIM)
    w1b = jnp.einsum("khmeo,gf->khmgefo", w1r, eye).reshape(2, 2, 16, 128, 128)
    w2k = jnp.zeros((NSA_KV_HEADS, 128, 128), F32)
    for g in range(NSA_KV_HEADS):
        w2k = w2k.at[g, g * 64:(g + 1) * 64, 0:64].set(w2[0])
    w2v = jnp.einsum("eo,gf->gefo", w2[1], eye).reshape(128, 128)
    return pe_t.astype(F32), w1b.astype(BF16), w2k.astype(BF16), w2v.astype(BF16)


def _compress_body(kc_ref, vc_ref, pe_ref, w1_ref, w2k_ref, w2v_ref, kcmp_ref, vcmpT_ref, *, nc):
    def hidden(x_ref, kv):
        a0 = jnp.zeros((nc, LANES), F32)
        a1 = jnp.zeros((nc, LANES), F32)
        for m in range(CMP_STRIDE):
            z = x_ref[0, pl.ds(m, nc, stride=CMP_STRIDE), :]
            a0 = a0 + jnp.dot((z + pe_ref[kv, 0, m]).astype(BF16), w1_ref[kv, 0, m],
                              preferred_element_type=F32)
            a1 = a1 + jnp.dot((z + pe_ref[kv, 1, m]).astype(BF16), w1_ref[kv, 1, m],
                              preferred_element_type=F32)
        hid = a0 + pltpu.roll(a1, nc - 1, 0)
        return jax.nn.silu(hid).astype(BF16)

    hk = hidden(kc_ref, 0)
    for g in range(NSA_KV_HEADS):
        kcmp_ref[0, g] = jnp.dot(hk, w2k_ref[g], preferred_element_type=F32).astype(BF16)
    hv = hidden(vc_ref, 1)
    vout = jnp.dot(hv, w2v_ref[...], preferred_element_type=F32)
    vcmpT_ref[0] = vout.T.astype(BF16)


def _compress(kc, vc, pe_t, w1b, w2k, w2v):
    B, T, _ = kc.shape
    nc = T // CMP_STRIDE
    full = lambda shape: pl.BlockSpec(shape, lambda b: (0,) * len(shape))
    return pl.pallas_call(
        functools.partial(_compress_body, nc=nc),
        grid=(B,),
        in_specs=[pl.BlockSpec((1, T, LANES), lambda b: (b, 0, 0)),
                  pl.BlockSpec((1, T, LANES), lambda b: (b, 0, 0)),
                  full((2, 2, 16, 1, 128)), full((2, 2, 16, 128, 128)), full((2, 128, 128)), full((128, 128))],
        out_specs=(pl.BlockSpec((1, 2, nc, 128), lambda b: (b, 0, 0, 0)),
                   pl.BlockSpec((1, 128, nc), lambda b: (b, 0, 0))),
        out_shape=(jax.ShapeDtypeStruct((B, 2, nc, 128), BF16),
                   jax.ShapeDtypeStruct((B, 128, nc), BF16)),
        compiler_params=_cparams(("arbitrary",)),
        name="nsa_compress",
    )(kc, vc, pe_t, w1b, w2k, w2v)


def _split_bf16(x):
    hi = x.astype(BF16)
    lo = (x - hi.astype(F32)).astype(BF16)
    return hi, lo


def _cmp_select_body(qa_ref, kcmp_ref, vcmpT_ref, ovT_ref, oc_ref, selb_ref, *, nc, qt, n_tiles_static):
    i = pl.program_id(1)
    q0 = i * qt

    def run(nrow, nblk):
        jj = lax.broadcasted_iota(jnp.int32, (nrow, qt), 0)
        tt = lax.broadcasted_iota(jnp.int32, (nrow, qt), 1) + q0
        cmask = (jj * CMP_STRIDE + (CMP_BLOCK - 1)) <= tt
        blk = lax.broadcasted_iota(jnp.int32, (nblk, qt), 0)
        ts = lax.broadcasted_iota(jnp.int32, (nblk, qt), 1) + q0
        cur = ts >> 6
        forced = (blk == 0) | (blk == cur) | (blk == cur - 1)
        causal = blk * SLC_BLOCK <= ts
        blkf = blk.astype(F32)
        ovT = ovT_ref[0:nblk, 0:nrow]

        raw = [lax.dot_general(kcmp_ref[0, h // NSA_GROUP, 0:nrow, :], qa_ref[0, h], _NT,
                               preferred_element_type=F32) for h in range(NSA_HEADS)]
        ocT = []
        for g in range(NSA_KV_HEADS):
            psum = jnp.zeros((nrow, qt), F32)
            for r in range(NSA_GROUP):
                h = g * NSA_GROUP + r
                s = jnp.where(cmask, raw[h], -jnp.inf)
                m = jnp.max(s, axis=0, keepdims=True)
                m = jnp.where(m > -jnp.inf, m, 0.0)
                e = jnp.exp2(s - m)
                den = jnp.sum(e, axis=0, keepdims=True)
                rden = 1.0 / jnp.maximum(den, 1e-30)
                ocT.append(jnp.dot(vcmpT_ref[0, g * 64:(g + 1) * 64, 0:nrow], e.astype(BF16),
                                   preferred_element_type=F32) * rden)
                psum = psum + e * rden
            hi, lo = _split_bf16(psum)
            imp = (jnp.dot(ovT, hi, preferred_element_type=F32)
                   + jnp.dot(ovT, lo, preferred_element_type=F32))
            imp = jnp.where(causal, imp, -1.0)
            imp = jnp.where(forced, -2.0, imp)
            sel = jnp.where(forced, 1.0, 0.0)
            for _ in range(SLC_TOPK - 3):
                mx = jnp.max(imp, axis=0, keepdims=True)
                first = jnp.min(jnp.where(imp == mx, blkf, float(N_SLC_PAD)), axis=0, keepdims=True)
                hit = blkf == first
                sel = jnp.where(hit, 1.0, sel)
                imp = jnp.where(hit, -2.0, imp)
            bias = jnp.where(sel > 0.5, 0.0, NEG)
            if nblk < N_SLC_PAD:
                bias = jnp.concatenate([bias, jnp.full((N_SLC_PAD - nblk, qt), NEG, F32)], axis=0)
            for sub in range(qt // Q_TILE):
                selb_ref[0, g, sub * Q_TILE:(sub + 1) * Q_TILE, :] = (
                    bias[:, sub * Q_TILE:(sub + 1) * Q_TILE].T.astype(BF16))
        for pr in range(NSA_HEADS // 2):
            pair = jnp.concatenate([ocT[2 * pr], ocT[2 * pr + 1]], axis=0)
            for sub in range(qt // Q_TILE):
                oc_ref[0, sub * Q_TILE:(sub + 1) * Q_TILE, pr * LANES:(pr + 1) * LANES] = (
                    pair[:, sub * Q_TILE:(sub + 1) * Q_TILE].T.astype(oc_ref.dtype))

    bounds = [(k * n_tiles_static) // CMP_CLASSES for k in range(CMP_CLASSES + 1)]
    for k in range(CMP_CLASSES):
        lo, hi = bounds[k], bounds[k + 1]
        if hi == lo:
            continue
        nrow = min(nc, -(-(hi * qt // CMP_STRIDE) // LANES) * LANES)
        nblk = min(N_SLC_PAD, max(32, -(-(hi * qt // SLC_BLOCK) // 8) * 8))
        pl.when((i >= lo) & (i < hi))(functools.partial(run, nrow, nblk))


def _overlap_T(nc):
    j = np.arange(nc)[None, :] * CMP_STRIDE
    s = np.arange(N_SLC_PAD)[:, None] * SLC_BLOCK
    ov = (j < s + SLC_BLOCK) & (j + CMP_BLOCK > s) & (np.arange(nc)[None, :] < nc - 1)
    return jnp.asarray(ov.astype(np.float32)).astype(BF16)


def _cmp_select(qa, kcmp, vcmpT):
    B, _, T, _ = qa.shape
    nc = T // CMP_STRIDE
    qt = CMP_Q_TILE
    return pl.pallas_call(
        functools.partial(_cmp_select_body, nc=nc, qt=qt, n_tiles_static=T // qt),
        grid=(B, T // qt),
        in_specs=[pl.BlockSpec((1, 6, qt, LANES), lambda b, i: (b, 0, i, 0)),
                  pl.BlockSpec((1, 2, nc, LANES), lambda b, i: (b, 0, 0, 0)),
                  pl.BlockSpec((1, LANES, nc), lambda b, i: (b, 0, 0)),
                  pl.BlockSpec((N_SLC_PAD, nc), lambda b, i: (0, 0))],
        out_specs=(pl.BlockSpec((1, qt, NSA_W), lambda b, i: (b, i, 0)),
                   pl.BlockSpec((1, 2, qt, N_SLC_PAD), lambda b, i: (b, 0, i, 0))),
        out_shape=(jax.ShapeDtypeStruct((B, T, NSA_W), BF16),
                   jax.ShapeDtypeStruct((B, 2, T, N_SLC_PAD), BF16)),
        compiler_params=_cparams(("arbitrary", "arbitrary")),
        name="nsa_cmp_select",
    )(qa, kcmp, vcmpT, _overlap_T(nc))


M_INIT = -1e29


def _store_heads_T(acc_ref, o_ref, qs):
    for sub in range(qs // Q_TILE):
        outs = []
        for g in range(NSA_KV_HEADS):
            for r in range(NSA_GROUP):
                lo = r * qs + sub * Q_TILE
                a = acc_ref[g, :, lo:lo + Q_TILE]
                outs.append(a[0:HEAD_DIM, :] / a[HEAD_DIM:HEAD_DIM + 1, :])
        for pr in range(NSA_HEADS // 2):
            pair = jnp.concatenate([outs[2 * pr], outs[2 * pr + 1]], axis=0)
            o_ref[0, sub * Q_TILE:(sub + 1) * Q_TILE, pr * LANES:(pr + 1) * LANES] = pair.T.astype(o_ref.dtype)


def _slc_body(qa_ref, selb_ref, kaug_ref, vspT_ref, o_ref, acc_ref, s_ref, p_ref, al_ref, qT_ref, *, qs):
    i = pl.program_id(1)
    q0 = i * qs
    rows = NSA_GROUP * qs
    nfull = q0 // KV_TILE
    t_len = kaug_ref.shape[2]
    for g in range(NSA_KV_HEADS):
        bias = selb_ref[0, g]
        for r in range(NSA_GROUP):
            qb = jnp.concatenate([bias, qa_ref[0, g * NSA_GROUP + r]], axis=1)
            for c in range(qs // MXU_COLS):
                lo = c * MXU_COLS
                qT_ref[g, r, :, lo:lo + MXU_COLS] = qb[lo:lo + MXU_COLS, :].T
    acc_ref[...] = jnp.zeros(acc_ref.shape, F32)
    s_ref[...] = jnp.full(s_ref.shape, NEG, F32)
    p_ref[...] = jnp.zeros(p_ref.shape, BF16)
    al_ref[...] = jnp.ones(al_ref.shape, F32)
    kpos = lax.broadcasted_iota(jnp.int32, (KV_TILE, MXU_COLS), 0)
    tq = lax.broadcasted_iota(jnp.int32, (KV_TILE, MXU_COLS), 1) + q0

    def trip(j, par, ms, qk, softmax):
        st_c = pl.multiple_of(jnp.maximum(j - 2, 0) * KV_TILE, KV_TILE)
        st_a = pl.multiple_of(jnp.minimum(j * KV_TILE, t_len - KV_TILE), KV_TILE)
        out = []
        for g in range(NSA_KV_HEADS):
            if qk is not None:
                kt = kaug_ref[0, g, pl.ds(st_a, KV_TILE), :]
                for r in range(NSA_GROUP):
                    for c in range(qs // MXU_COLS):
                        lo = c * MXU_COLS
                        s = jnp.dot(kt, qT_ref[g, r, :, lo:lo + MXU_COLS], preferred_element_type=F32)
                        if qk == "masked":
                            s = jnp.where(kpos + j * KV_TILE <= tq + lo, s, NEG)
                        s_ref[par, g, :, r * qs + lo:r * qs + lo + MXU_COLS] = s
        for g in range(NSA_KV_HEADS):
            vt = vspT_ref[0, g, :, pl.ds(st_c, KV_TILE)]
            for c in range(rows // MXU_COLS):
                cols = slice(c * MXU_COLS, (c + 1) * MXU_COLS)
                acc_ref[g, :, cols] = al_ref[par, g, :, cols] * acc_ref[g, :, cols] + jnp.dot(
                    vt, p_ref[par, g, :, cols], preferred_element_type=F32)
        for g in range(NSA_KV_HEADS):
            m_new = ms[g]
            if softmax:
                s = s_ref[1 - par, g]
                m_new = jnp.maximum(ms[g], jnp.max(s, axis=0, keepdims=True))
                al_ref[1 - par, g] = jnp.exp2(ms[g] - m_new)
                p_ref[1 - par, g] = jnp.exp2(s - m_new).astype(BF16)
            out.append(m_new)
        return tuple(out)

    def pair(k, ms):
        ms = trip(2 * k, 0, ms, "plain", True)
        return trip(2 * k + 1, 1, ms, "plain", True)

    def quad(k, ms):
        return pair(2 * k + 1, pair(2 * k, ms))

    npairs = nfull // 2
    nquads = npairs // 2
    ms = tuple(jnp.full((1, rows), M_INIT, F32) for _ in range(NSA_KV_HEADS))
    ms = lax.fori_loop(0, nquads, quad, ms)
    ms = lax.fori_loop(2 * nquads, npairs, pair, ms)
    t0 = 2 * npairs

    def tail(nreal):
        def run(ms):
            for d in range(nreal):
                ms = trip(t0 + d, d % 2, ms, "masked", True)
            ms = trip(t0 + nreal, nreal % 2, ms, None, True)
            trip(t0 + nreal + 1, (nreal + 1) % 2, ms, None, False)
            return 0
        return run

    lax.cond(nfull > t0, tail(2), tail(1), ms)
    _store_heads_T(acc_ref, o_ref, qs)


def _slc_attn(qa, selb, kaug, vspT, qs=SLC_Q_TILE):
    B, _, T, _ = qa.shape
    assert T % qs == 0 and KV_TILE % qs == 0
    rows = NSA_GROUP * qs
    return pl.pallas_call(
        functools.partial(_slc_body, qs=qs),
        grid=(B, T // qs),
        in_specs=[pl.BlockSpec((1, 6, qs, LANES), lambda b, i: (b, 0, i, 0)),
                  pl.BlockSpec((1, 2, qs, N_SLC_PAD), lambda b, i: (b, 0, i, 0)),
                  pl.BlockSpec((1, 2, T, 2 * LANES), lambda b, i: (b, 0, 0, 0)),
                  pl.BlockSpec((1, 2, V_ROWS, T), lambda b, i: (b, 0, 0, 0))],
        out_specs=pl.BlockSpec((1, qs, NSA_W), lambda b, i: (b, i, 0)),
        out_shape=jax.ShapeDtypeStruct((B, T, NSA_W), BF16),
        scratch_shapes=[pltpu.VMEM((NSA_KV_HEADS, V_ROWS, rows), F32),
                        pltpu.VMEM((2, NSA_KV_HEADS, KV_TILE, rows), F32),
                        pltpu.VMEM((2, NSA_KV_HEADS, KV_TILE, rows), BF16),
                        pltpu.VMEM((2, NSA_KV_HEADS, 1, rows), F32),
                        pltpu.VMEM((NSA_KV_HEADS, NSA_GROUP, 2 * LANES, qs), BF16)],
        compiler_params=_cparams(("arbitrary", "arbitrary")),
        name="nsa_slc_attn",
    )(qa, selb, kaug, vspT)


def _win_body(qa_ref, kwp_ref, vwT_ref, o_ref, acc_ref, *, qs):
    i = pl.program_id(1)
    q0 = i * qs
    rows = NSA_GROUP * qs
    ntile = WIN // qs + 1
    kk = lax.broadcasted_iota(jnp.int32, (qs, rows), 0)
    tq = (lax.broadcasted_iota(jnp.int32, (qs, rows), 1) & (qs - 1)) + q0
    starts = [pl.multiple_of(jnp.maximum(i - (ntile - 1) + d, 0) * qs, qs) for d in range(ntile)]
    raw = []
    for g in range(NSA_KV_HEADS):
        qrows = jnp.concatenate([qa_ref[0, g * NSA_GROUP + r] for r in range(NSA_GROUP)], axis=0)
        raw.append([lax.dot_general(kwp_ref[0, g, pl.ds(starts[d], qs), :], qrows, _NT,
                                    preferred_element_type=F32) for d in range(ntile)])
    for g in range(NSA_KV_HEADS):
        scores = []
        for d in range(ntile):
            kpos = kk + (i - (ntile - 1) + d) * qs
            if d == 0:
                mask = (kpos >= 0) & ((tq - kpos) < WIN)
            elif d == ntile - 1:
                mask = kpos <= tq
            else:
                mask = kpos >= 0
            scores.append(jnp.where(mask, raw[g][d], NEG))
        m = functools.reduce(jnp.maximum, [jnp.max(s, axis=0, keepdims=True) for s in scores])
        acc = jnp.zeros((V_ROWS, rows), F32)
        for s, st in zip(scores, starts):
            acc = acc + jnp.dot(vwT_ref[0, g, :, pl.ds(st, qs)], jnp.exp2(s - m).astype(BF16),
                                preferred_element_type=F32)
        acc_ref[g] = acc
    _store_heads_T(acc_ref, o_ref, qs)


def _win_attn(qa, kwp, vwT, qs=WIN_Q_TILE):
    B, _, T, _ = qa.shape
    assert WIN % qs == 0 and T % qs == 0
    rows = NSA_GROUP * qs
    return pl.pallas_call(
        functools.partial(_win_body, qs=qs),
        grid=(B, T // qs),
        in_specs=[pl.BlockSpec((1, 6, qs, LANES), lambda b, i: (b, 0, i, 0)),
                  pl.BlockSpec((1, 2, T, LANES), lambda b, i: (b, 0, 0, 0)),
                  pl.BlockSpec((1, 2, V_ROWS, T), lambda b, i: (b, 0, 0, 0))],
        out_specs=pl.BlockSpec((1, qs, NSA_W), lambda b, i: (b, i, 0)),
        out_shape=jax.ShapeDtypeStruct((B, T, NSA_W), BF16),
        scratch_shapes=[pltpu.VMEM((NSA_KV_HEADS, V_ROWS, rows), F32)],
        compiler_params=_cparams(("arbitrary", "arbitrary")),
        name="nsa_win_attn",
    )(qa, kwp, vwT)


DIL_CHUNK = 2048
DIL_UNROLL = 4


def _dil_body(q_ref, kp_ref, kc_ref, vp_ref, vc_ref, o_ref, l_ref, kbuf, vbuf, *, dil):
    i = pl.program_id(1)
    span = DIL_BLOCK * dil
    nsb = DIL_CHUNK // span
    kbuf[0:span, :] = kp_ref[0]
    kbuf[span:span + DIL_CHUNK, :] = kc_ref[0]
    vbuf[0:span, :] = vp_ref[0]
    vbuf[span:span + DIL_CHUNK, :] = vc_ref[0]
    a = lax.broadcasted_iota(jnp.int32, (DIL_BLOCK, 2 * DIL_BLOCK), 0)
    c = lax.broadcasted_iota(jnp.int32, (DIL_BLOCK, 2 * DIL_BLOCK), 1)
    dist = a + DIL_BLOCK - c
    band = (dist >= 0) & (dist <= DIL_BLOCK)
    head0 = lax.broadcasted_iota(jnp.int32, (DIL_BLOCK, LANES), 1) < HEAD_DIM
    shift = dil.bit_length() - 1

    def trip(t, carry):
        work = []
        for k in range(DIL_UNROLL):
            u = t * DIL_UNROLL + k
            sb = u >> shift
            r = u & (dil - 1)
            row0 = sb * span + r
            if dil == 1:
                rows_q = pl.ds(row0, DIL_BLOCK)
                rows_kv = pl.ds(row0, 2 * DIL_BLOCK)
            else:
                rows_q = pl.ds(row0, DIL_BLOCK, stride=dil)
                rows_kv = pl.ds(row0, 2 * DIL_BLOCK, stride=dil)
            q = q_ref[0, rows_q, :]
            kk = kbuf[rows_kv, :].astype(BF16)
            scs = [lax.dot_general(jnp.where(hsel, q, 0.0).astype(BF16), kk, _NT, preferred_element_type=F32)
                   for hsel in (head0, jnp.logical_not(head0))]
            work.append((sb, rows_q, rows_kv, scs))
        for sb, rows_q, rows_kv, scs in work:
            vv = vbuf[rows_kv, :].astype(BF16)
            mask = band & ((c + (i * nsb + sb - 1) * DIL_BLOCK) >= 0)
            o_h, l_h = [], []
            for sc in scs:
                sc = jnp.where(mask, sc, NEG)
                m = jnp.max(sc, axis=1, keepdims=True)
                e = jnp.exp(sc - m)
                den = jnp.sum(e, axis=1, keepdims=True)
                pv = jnp.dot(e.astype(BF16), vv, preferred_element_type=F32)
                o_h.append(pv / den)
                l_h.append(jnp.broadcast_to(m + jnp.log(den), (DIL_BLOCK, LANES)))
            o_ref[0, rows_q, :] = jnp.where(head0, o_h[0], o_h[1])
            l_ref[0, rows_q, :] = jnp.where(head0, l_h[0], l_h[1])
        return carry

    lax.fori_loop(0, DIL_CHUNK // DIL_BLOCK // DIL_UNROLL, trip, 0)


def _dil_attn(dq, dk, dv, gi, dil):
    B, T, _ = dq.shape
    span = DIL_BLOCK * dil
    assert T % DIL_CHUNK == 0 and DIL_CHUNK % span == 0
    cur = pl.BlockSpec((1, DIL_CHUNK, LANES), lambda b, i: (b, i, gi))
    prev = pl.BlockSpec((1, span, LANES), lambda b, i: (b, jnp.maximum(i * (DIL_CHUNK // span) - 1, 0), gi))
    out = pl.BlockSpec((1, DIL_CHUNK, LANES), lambda b, i: (b, i, 0))
    return pl.pallas_call(
        functools.partial(_dil_body, dil=dil),
        grid=(B, T // DIL_CHUNK),
        in_specs=[cur, prev, cur, prev, cur],
        out_specs=(out, out),
        out_shape=(jax.ShapeDtypeStruct((B, T, LANES), F32), jax.ShapeDtypeStruct((B, T, LANES), F32)),
        scratch_shapes=[pltpu.VMEM((span + DIL_CHUNK, LANES), F32), pltpu.VMEM((span + DIL_CHUNK, LANES), F32)],
        compiler_params=_cparams(("arbitrary", "arbitrary")),
        name="dilated_attn",
    )(dq, dk, dk, dv, dv)


def _dilated(dq, dk, dv):
    outs, lses = [], []
    for gi, (window, dil) in enumerate(DIL_PAIRS):
        assert window == DIL_BLOCK * dil
        o, l = _dil_attn(dq, dk, dv, gi, dil)
        outs.append(o)
        lses.append(l)
    return outs, lses


def _gate_expand():
    m = np.zeros((3, 2 * LANES, NSA_W), np.float32)
    for j in range(3):
        for h in range(NSA_HEADS):
            m[j, h * 3 + j, h * HEAD_DIM:(h + 1) * HEAD_DIM] = 1.0
            m[j, LANES + h * 3 + j, h * HEAD_DIM:(h + 1) * HEAD_DIM] = 1.0
    return jnp.asarray(m).astype(BF16)


def _out_body(h_ref, conv_ref, halo_ref, oc_ref, os_ref, ow_ref, ng_ref, nz_ref,
              o0_ref, o1_ref, o2_ref, l0_ref, l1_ref, l2_ref, dz_ref, p_ref,
              wout_ref, wgate_ref, wproj_ref, cw_ref, cb_ref, gexp_ref, nf_ref, out_ref, *, tm, final):
    i = pl.program_id(1)
    cv = conv_ref[0]
    a_b, a_c, a_h, a_z = cv[:, 0:256], cv[:, 256:512], cv[:, 512:768], cv[:, 768:1024]
    u = a_c * a_h
    hl = halo_ref[0]
    uh = hl[:, 256:512] * hl[:, 512:768] * (i > 0).astype(F32)
    rowi = lax.broadcasted_iota(jnp.int32, (tm, CONV_W), 0)
    u1 = jnp.where(rowi == 0, uh[7:8, :], pltpu.roll(u, 1, 0))
    u2 = jnp.where(rowi == 0, uh[6:7, :], jnp.where(rowi == 1, uh[7:8, :], pltpu.roll(u, 2, 0)))
    cw = cw_ref[...]
    conv = cw[0:1, :] * u2 + cw[1:2, :] * u1 + cw[2:3, :] * u + cb_ref[...]
    y_a = a_b * conv * jax.nn.silu(a_z)
    gates = jax.nn.sigmoid(ng_ref[0])
    ghl = jnp.concatenate(_split_bf16(gates), axis=1)
    y_b = jnp.zeros((tm, NSA_W), F32)
    for j, o_ref in enumerate((oc_ref, os_ref, ow_ref)):
        gj = jnp.dot(ghl, gexp_ref[j], preferred_element_type=F32)
        y_b = y_b + gj * o_ref[0].astype(F32)
    y_b = y_b * jax.nn.silu(nz_ref[0])
    l0, l1, l2 = l0_ref[0], l1_ref[0], l2_ref[0]
    mx = jnp.maximum(jnp.maximum(l0, l1), l2)
    e0, e1, e2 = jnp.exp(l0 - mx), jnp.exp(l1 - mx), jnp.exp(l2 - mx)
    den = e0 + e1 + e2
    y_c = jnp.concatenate([(e0 / den) * o0_ref[0], (e1 / den) * o1_ref[0], (e2 / den) * o2_ref[0]], axis=1)
    y_c = y_c * jax.nn.silu(dz_ref[0])
    y = jnp.concatenate([y_a, y_b, y_c], axis=1).astype(BF16)
    h1 = h_ref[0] + jnp.dot(y, wout_ref[...], preferred_element_type=F32)
    gate = jax.nn.sigmoid(jnp.dot(h1.astype(BF16), wgate_ref[...], preferred_element_type=F32))
    pp = jnp.dot(p_ref[0, 0].astype(BF16), wproj_ref[...], preferred_element_type=F32)
    h2 = h1 + gate * pp
    if final:
        ms = jnp.mean(h2 * h2, axis=-1, keepdims=True)
        h2 = h2 * lax.rsqrt(ms + RMS_EPS) * nf_ref[...]
    out_ref[0] = h2


def _out_proj(h, conv, oc, os_, ow, ng, nz, outs, lses, dz, p, layer, wout, wgate, wproj, cw, cb, nf, final):
    B, T, D = h.shape
    tm = min(ROW_TILE, T)
    row = lambda n: pl.BlockSpec((1, tm, n), lambda b, i: (b, i, 0))
    full = lambda shape: pl.BlockSpec(shape, lambda b, i: (0,) * len(shape))
    halo = pl.BlockSpec((1, 8, 1024), lambda b, i: (b, jnp.maximum(i * (tm // 8) - 1, 0), 0))
    ple = pl.BlockSpec((1, 1, tm, PLE_DIM), lambda b, i: (layer, b, i, 0))
    in_specs = [row(D), row(1024), halo, row(384), row(384), row(384), row(128), row(384),
                row(128), row(128), row(128), row(128), row(128), row(128), row(384), ple,
                full((D, D)), full((D, D)), full((PLE_DIM, D)), full((8, CONV_W)), full((1, CONV_W)),
                full((3, 2 * LANES, NSA_W)), full((1, D))]
    return pl.pallas_call(
        functools.partial(_out_body, tm=tm, final=final),
        grid=(B, T // tm),
        in_specs=in_specs,
        out_specs=row(D),
        out_shape=jax.ShapeDtypeStruct((B, T, D), F32),
        compiler_params=_cparams(("arbitrary", "arbitrary")),
        name="out_proj",
    )(h, conv, conv, oc, os_, ow, ng, nz, *outs, *lses, dz, p, wout, wgate, wproj, cw, cb,
      _gate_expand(), nf)


def kernel(x, p, norm_mix, w_in, conv_w, conv_b, cmp_pe, cmp_w1, cmp_w2, w_out, w_ple_gate, w_ple_proj,
           norm_final):
    B, T, D = x.shape
    depth = w_in.shape[0]
    assert D == D_MODEL and T % 2048 == 0 and T // SLC_BLOCK <= N_SLC_PAD
    h = x
    for i in range(depth):
        w = _permute_w_in(w_in[i])
        (conv, qa, kaug, vspT, kwp, vwT, kc, vc, ng, nz, dq, dk, dv, dz) = _inproj(
            h, norm_mix[i].reshape(1, D), w)
        kcmp, vcmpT = _compress(kc, vc, *_compress_weights(cmp_pe[i], cmp_w1[i], cmp_w2[i]))
        oc, selb = _cmp_select(qa, kcmp, vcmpT)
        os_ = _slc_attn(qa, selb, kaug, vspT)
        ow = _win_attn(qa, kwp, vwT)
        outs, lses = _dilated(dq, dk, dv)
        cw = jnp.zeros((8, CONV_W), F32).at[0:CONV_K].set(conv_w[i])
        h = _out_proj(h, conv, oc, os_, ow, ng, nz, outs, lses, dz, p, i,
                      w_out[i].astype(BF16), w_ple_gate[i].astype(BF16), w_ple_proj[i].astype(BF16),
                      cw, conv_b[i].reshape(1, CONV_W), norm_final.reshape(1, D), final=(i == depth - 1))
    return h
```
